```python
import math
import jax, jax.numpy as jnp
from jax import lax
import numpy as np

D_MODEL = 1024
BATCH = 2
SEQ = 8192
DEPTH = 4

GRID_W = 64
CTX_LEN = 256
N_MIXERS = 3
EPS = 1e-6

CHUNK = 128
SG_WIDTH = 2 * D_MODEL
SG_GROUPS = 8

HY_ORDER = 2
HY_SHORT = 3
HY_POS_EMB = 33
HY_FILTER_HIDDEN = 64
HY_TARGET = 1e-2
HY_FAST_DECAY_PCT = 0.3
HY_SLOW_DECAY_PCT = 1.5
HY_MAX_DECAY = math.log(HY_TARGET) / HY_FAST_DECAY_PCT
HY_MIN_DECAY = math.log(HY_TARGET) / HY_SLOW_DECAY_PCT

DA_HEADS = 8
DA_HEAD_DIM = D_MODEL // (2 * DA_HEADS)
DA_V_DIM = 2 * DA_HEAD_DIM
DA_QK = DA_HEADS * 2 * DA_HEAD_DIM
Q_BLOCK = 128
ROPE_HALF = DA_HEAD_DIM // 4
ROPE_BASE = 10000.0

N_EXPERTS = 16
EC_CAPACITY = 2
EXPERT_FF = 2 * D_MODEL

N_SG = (DEPTH + 2) // 3
N_HY = (DEPTH + 1) // 3
N_DA = DEPTH // 3

kernel_name = 'hybrid_diffusion_sg_hyena_diffattn_ecmoe'


def rms_norm(h, eps=EPS):
    hf = h.astype(jnp.float32)
    return (hf * lax.rsqrt(jnp.mean(hf * hf, axis=-1, keepdims=True) + eps)).astype(h.dtype)


def modulation(cond, w, b):
    m = jax.nn.silu(cond) @ w + b
    return jnp.split(m[..., None, :], 6, axis=-1)


def ada_norm(h, shift, scale):
    return rms_norm(h) * (1.0 + scale) + shift


def spatial_gating_mixer(h, w_in, b_in, v_gain, w_s, b_s, w_out, b_out):
    bsz, n, _ = h.shape
    zz = jax.nn.gelu(h @ w_in + b_in, approximate=False)
    u, v = jnp.split(zz, 2, axis=-1)
    v = rms_norm(v) * v_gain
    vb = v.reshape(bsz, n // CHUNK, CHUNK, SG_GROUPS, SG_WIDTH // SG_GROUPS)
    s = jnp.einsum('gpq,bnqgc->bnpgc', w_s, vb) + jnp.swapaxes(b_s, 0, 1)[:, :, None]
    return (u * s.reshape(bsz, n, SG_WIDTH)) @ w_out + b_out


def hyena_filter_spectra(n, w1, b1, w2, b2, w3, b3, freq, w_out):
    f32 = jnp.float32
    t = jnp.linspace(0.0, 1.0, n, dtype=f32)[:, None]
    bands = (HY_POS_EMB - 1) // 2
    f = jnp.linspace(1e-4, bands - 1, bands, dtype=f32)
    w = (2.0 * math.pi / n) * jnp.arange(n, dtype=f32)[:, None]
    z = jnp.concatenate([t, jnp.cos(f * w), -jnp.sin(f * w)], axis=-1)
    fr = freq.astype(f32)
    hid = jnp.sin(fr * (z @ w1.astype(f32) + b1.astype(f32)))
    hid = jnp.sin(fr * (hid @ w2.astype(f32) + b2.astype(f32)))
    hid = jnp.sin(fr * (hid @ w3.astype(f32) + b3.astype(f32)))
    h = (hid @ w_out.astype(f32)).reshape(n, HY_ORDER, 2, D_MODEL)
    deltas = jnp.abs(jnp.linspace(HY_MIN_DECAY, HY_MAX_DECAY, D_MODEL, dtype=f32))
    h = h * jnp.exp(-t * deltas)[:, None, None, :]
    full = jnp.concatenate([h[:, :, 0], jnp.zeros((1, HY_ORDER, D_MODEL), f32), h[:0:-1, :, 1]], axis=0)
    full = full * lax.rsqrt(jnp.sum(full * full, axis=0, keepdims=True) + EPS)
    return jnp.fft.rfft(full, axis=0)


def long_conv(z, spec):
    n = z.shape[1]
    return jnp.fft.irfft(jnp.fft.rfft(z, n=2 * n, axis=1) * spec, n=2 * n, axis=1)[:, :n]


def short_conv(u, w, b):
    n = u.shape[1]
    pad = HY_SHORT // 2
    up = jnp.pad(u, ((0, 0), (pad, pad), (0, 0)))
    out = up[:, 0:n] * w[0]
    for k in range(1, HY_SHORT):
        out = out + up[:, k:k + n] * w[k]
    return out + b


def hyena_mixer(h, w_in, b_in, conv_w, conv_b, f_w1, f_b1, f_w2, f_b2, f_w3, f_b3, f_freq, f_wout, skip, w_out, b_out):
    n = h.shape[1]
    spec = hyena_filter_spectra(n, f_w1, f_b1, f_w2, f_b2, f_w3, f_b3, f_freq, f_wout)
    u = short_conv(h @ w_in + b_in, conv_w, conv_b)
    v, x1, x2 = jnp.split(u, 3, axis=-1)
    z = v.astype(jnp.float32)
    for o, gate in enumerate((x1, x2)):
        z = gate.astype(jnp.float32) * (long_conv(z, spec[:, o]) + skip[o].astype(jnp.float32) * z)
    return z.astype(h.dtype) @ w_out + b_out


def axial_rope_tables(n):
    rows = n // GRID_W
    row = jnp.broadcast_to(jnp.arange(rows, dtype=jnp.float32)[:, None], (rows, GRID_W)).reshape(n)
    col = jnp.broadcast_to(jnp.arange(GRID_W, dtype=jnp.float32)[None, :], (rows, GRID_W)).reshape(n)
    inv = ROPE_BASE ** (-jnp.arange(ROPE_HALF, dtype=jnp.float32) / ROPE_HALF)
    ang = jnp.stack([row[:, None] * inv, col[:, None] * inv], axis=1)
    return jnp.cos(ang), jnp.sin(ang)


def apply_axial_rope(x, cos, sin):
    xr = x.reshape(*x.shape[:-1], 2, 2, ROPE_HALF)
    x1, x2 = xr[..., 0, :], xr[..., 1, :]
    c = cos.astype(x.dtype)[:, None, None]
    s = sin.astype(x.dtype)[:, None, None]
    return jnp.stack([x1 * c - x2 * s, x2 * c + x1 * s], axis=-2).reshape(x.shape)


def diff_attn_block(q, k, v, lam):
    s = jnp.einsum('bqhmd,bkhmd->bhmqk', q, k).astype(jnp.float32) * (DA_HEAD_DIM ** -0.5)
    p = jax.nn.softmax(s, axis=-1)
    a = p[:, :, 0] - lam * p[:, :, 1]
    return jnp.einsum('bhqk,bkhe->bqhe', a.astype(v.dtype), v)


def diff_attention(hx, hc, ctx_out, cos, sin, layer_idx, w_in, q_gain, k_gain, lam_q1, lam_k1, lam_q2, lam_k2, subln_gain, w_out):
    f32 = jnp.float32
    lam_init = 0.8 - 0.6 * math.exp(-0.3 * layer_idx)
    lam = (jnp.exp(jnp.sum(lam_q1.astype(f32) * lam_k1.astype(f32)))
           - jnp.exp(jnp.sum(lam_q2.astype(f32) * lam_k2.astype(f32))) + lam_init)

    def project(h):
        bsz, n, _ = h.shape
        q, k, v = jnp.split(h @ w_in, [DA_QK, 2 * DA_QK], axis=-1)
        q = rms_norm(q.reshape(bsz, n, DA_HEADS, 2, DA_HEAD_DIM)) * q_gain
        k = rms_norm(k.reshape(bsz, n, DA_HEADS, 2, DA_HEAD_DIM)) * k_gain
        return q, k, v.reshape(bsz, n, DA_HEADS, DA_V_DIM)

    def merge(o):
        bsz, n = o.shape[:2]
        o = rms_norm(o) * subln_gain * (1.0 - lam_init)
        return o.reshape(bsz, n, DA_HEADS * DA_V_DIM) @ w_out

    qx, kx, vx = project(hx)
    qx = apply_axial_rope(qx, cos, sin)
    kx = apply_axial_rope(kx, cos, sin)
    qc, kc, vc = project(hc)
    k_all = jnp.concatenate([kc, kx], axis=1)
    v_all = jnp.concatenate([vc, vx], axis=1)
    bsz, n = qx.shape[:2]
    qb = jnp.moveaxis(qx.reshape(bsz, n // Q_BLOCK, Q_BLOCK, DA_HEADS, 2, DA_HEAD_DIM), 1, 0)
    ox = lax.map(lambda qblk: diff_attn_block(qblk, k_all, v_all, lam), qb)
    ox = jnp.moveaxis(ox, 0, 1).reshape(bsz, n, DA_HEADS, DA_V_DIM)
    yx = merge(ox)
    yc = merge(diff_attn_block(qc, kc, vc, lam)) if ctx_out else None
    return yx, yc


def ec_moe(h, w_router, w_gate, w_up, w_down):
    bsz, n, d = h.shape
    cap = EC_CAPACITY * n // N_EXPERTS
    aff = jax.nn.softmax((h @ w_router).astype(jnp.float32), axis=-1)
    g, idx = lax.top_k(jnp.swapaxes(aff, 1, 2), cap)
    xs = jax.vmap(lambda hb, ib: hb[ib])(h, idx)
    a = jnp.einsum('becd,edf->becf', xs, w_gate)
    b = jnp.einsum('becd,edf->becf', xs, w_up)
    ye = jnp.einsum('becf,efd->becd', jax.nn.silu(a) * b, w_down) * g[..., None].astype(h.dtype)
    return jax.vmap(lambda yb, ib: jnp.zeros((n, d), h.dtype).at[ib.reshape(-1)].add(yb.reshape(-1, d)))(ye, idx)


def setup_inputs(seed: int = 0) -> dict:
    key = jax.random.key(seed)
    ks = iter(jax.random.split(key, 64))
    f32 = jnp.float32
    D = D_MODEL

    def nrm(shape, scale):
        return jax.random.normal(next(ks), shape, f32) * scale

    def gain(shape):
        return 1.0 + nrm(shape, 0.02)

    HID = HY_FILTER_HIDDEN
    return {
        'x': nrm((BATCH, SEQ, D), 1.0),
        'c': nrm((BATCH, D), 1.0),
        'ctx': nrm((BATCH, CTX_LEN, D), 1.0),
        'c_ctx': nrm((D,), 1.0),
        'ada_w': nrm((DEPTH, D, 6 * D), 0.5 * D ** -0.5),
        'ada_b': nrm((DEPTH, 6 * D), 0.02),
        'sg_w_in': nrm((N_SG, D, 2 * SG_WIDTH), D ** -0.5),
        'sg_b_in': nrm((N_SG, 2 * SG_WIDTH), 0.02),
        'sg_v_gain': gain((N_SG, SG_WIDTH)),
        'sg_w_s': nrm((N_SG, SG_GROUPS, CHUNK, CHUNK), CHUNK ** -0.5),
        'sg_b_s': gain((N_SG, SG_GROUPS, CHUNK)),
        'sg_w_out': nrm((N_SG, SG_WIDTH, D), SG_WIDTH ** -0.5),
        'sg_b_out': nrm((N_SG, D), 0.02),
        'hy_w_in': nrm((N_HY, D, 3 * D), D ** -0.5),
        'hy_b_in': nrm((N_HY, 3 * D), 0.02),
        'hy_conv_w': nrm((N_HY, HY_SHORT, 3 * D), HY_SHORT ** -0.5),
        'hy_conv_b': nrm((N_HY, 3 * D), 0.02),
        'hy_f_w1': nrm((N_HY, HY_POS_EMB, HID), HY_POS_EMB ** -0.5),
        'hy_f_b1': nrm((N_HY, HID), 0.02),
        'hy_f_w2': nrm((N_HY, HID, HID), HID ** -0.5),
        'hy_f_b2': nrm((N_HY, HID), 0.02),
        'hy_f_w3': nrm((N_HY, HID, HID), HID ** -0.5),
        'hy_f_b3': nrm((N_HY, HID), 0.02),
        'hy_f_freq': gain((N_HY, HID)),
        'hy_f_wout': nrm((N_HY, HID, HY_ORDER * 2 * D), HID ** -0.5),
        'hy_skip': nrm((N_HY, HY_ORDER, D), 0.5),
        'hy_w_out': nrm((N_HY, D, D), D ** -0.5),
        'hy_b_out': nrm((N_HY, D), 0.02),
        'da_w_in': nrm((N_DA, D, 2 * DA_QK + DA_HEADS * DA_V_DIM), D ** -0.5),
        'da_q_gain': gain((N_DA, DA_HEAD_DIM)),
        'da_k_gain': gain((N_DA, DA_HEAD_DIM)),
        'da_lam_q1': nrm((N_DA, DA_HEAD_DIM), 0.1),
        'da_lam_k1': nrm((N_DA, DA_HEAD_DIM), 0.1),
        'da_lam_q2': nrm((N_DA, DA_HEAD_DIM), 0.1),
        'da_lam_k2': nrm((N_DA, DA_HEAD_DIM), 0.1),
        'da_subln_gain': gain((N_DA, DA_V_DIM)),
        'da_w_out': nrm((N_DA, DA_HEADS * DA_V_DIM, D), (DA_HEADS * DA_V_DIM) ** -0.5),
        'moe_router': nrm((DEPTH, D, N_EXPERTS), D ** -0.5),
        'moe_w_gate': nrm((DEPTH, N_EXPERTS, D, EXPERT_FF), D ** -0.5),
        'moe_w_up': nrm((DEPTH, N_EXPERTS, D, EXPERT_FF), D ** -0.5),
        'moe_w_down': nrm((DEPTH, N_EXPERTS, EXPERT_FF, D), EXPERT_FF ** -0.5),
    }


def reference(x, c, ctx, c_ctx, ada_w, ada_b,
              sg_w_in, sg_b_in, sg_v_gain, sg_w_s, sg_b_s, sg_w_out, sg_b_out,
              hy_w_in, hy_b_in, hy_conv_w, hy_conv_b, hy_f_w1, hy_f_b1, hy_f_w2, hy_f_b2, hy_f_w3, hy_f_b3,
              hy_f_freq, hy_f_wout, hy_skip, hy_w_out, hy_b_out,
              da_w_in, da_q_gain, da_k_gain, da_lam_q1, da_lam_k1, da_lam_q2, da_lam_k2, da_subln_gain, da_w_out,
              moe_router, moe_w_gate, moe_w_up, moe_w_down):
    cos, sin = axial_rope_tables(x.shape[1])
    for i in range(DEPTH):
        kind = i % N_MIXERS
        j = i // N_MIXERS
        ctx_live = any(l % N_MIXERS == 2 for l in range(i + 1, DEPTH))
        sh1, sc1, g1, sh2, sc2, g2 = modulation(c, ada_w[i], ada_b[i])
        hx = ada_norm(x, sh1, sc1)
        hc = None
        if kind == 2 or ctx_live:
            csh1, csc1, cg1, csh2, csc2, cg2 = modulation(c_ctx, ada_w[i], ada_b[i])
            hc = ada_norm(ctx, csh1, csc1)
        yc = None
        if kind == 0:
            sg_args = (sg_w_in[j], sg_b_in[j], sg_v_gain[j], sg_w_s[j], sg_b_s[j], sg_w_out[j], sg_b_out[j])
            yx = spatial_gating_mixer(hx, *sg_args)
            if ctx_live:
                yc = spatial_gating_mixer(hc, *sg_args)
        elif kind == 1:
            hy_args = (hy_w_in[j], hy_b_in[j], hy_conv_w[j], hy_conv_b[j], hy_f_w1[j], hy_f_b1[j],
                       hy_f_w2[j], hy_f_b2[j], hy_f_w3[j], hy_f_b3[j], hy_f_freq[j], hy_f_wout[j],
                       hy_skip[j], hy_w_out[j], hy_b_out[j])
            yx = hyena_mixer(hx, *hy_args)
            if ctx_live:
                yc = hyena_mixer(hc, *hy_args)
        else:
            yx, yc = diff_attention(hx, hc, ctx_live, cos, sin, i, da_w_in[j], da_q_gain[j], da_k_gain[j],
                                    da_lam_q1[j], da_lam_k1[j], da_lam_q2[j], da_lam_k2[j],
                                    da_subln_gain[j], da_w_out[j])
        x = x + g1 * yx
        x = x + g2 * ec_moe(ada_norm(x, sh2, sc2), moe_router[i], moe_w_gate[i], moe_w_up[i], moe_w_down[i])
        if ctx_live:
            ctx = ctx + cg1 * yc
            ctx = ctx + cg2 * ec_moe(ada_norm(ctx, csh2, csc2), moe_router[i], moe_w_gate[i], moe_w_up[i], moe_w_down[i])
    return x
```

```python
import functools
import math

import jax
import jax.numpy as jnp
from jax import lax
from jax.experimental import pallas as pl
from jax.experimental.pallas import tpu as pltpu

f32 = jnp.float32
bf16 = jnp.bfloat16
HIGHEST = lax.Precision.HIGHEST

EPS = 1e-6
N_MIXERS = 3
CHUNK = 128
SG_GROUPS = 8
HY_ORDER = 2
HY_SHORT = 3
HY_POS_EMB = 33
HY_TARGET = 1e-2
HY_FAST_DECAY_PCT = 0.3
HY_SLOW_DECAY_PCT = 1.5
HY_MAX_DECAY = math.log(HY_TARGET) / HY_FAST_DECAY_PCT
HY_MIN_DECAY = math.log(HY_TARGET) / HY_SLOW_DECAY_PCT
GRID_W = 64
DA_HEADS = 8
ROPE_BASE = 10000.0
EC_CAPACITY = 2

VMEM_LIMIT = 56 * 1024 * 1024


def _cparams(sem):
    return pltpu.CompilerParams(dimension_semantics=sem, vmem_limit_bytes=VMEM_LIMIT)


def _rms(x):
    return x * lax.rsqrt(jnp.mean(x * x, axis=-1, keepdims=True) + EPS)


def _mod_body(c_ref, w_ref, b_ref, o_ref):
    c = c_ref[...]
    s = c * (1.0 / (1.0 + jnp.exp(-c)))
    o_ref[0] = jnp.dot(s, w_ref[0], precision=HIGHEST, preferred_element_type=f32) + b_ref[0]


def _modulation(cond, ada_w, ada_b):
    depth, d, n6 = ada_w.shape
    tn = 1536
    return pl.pallas_call(
        _mod_body,
        out_shape=jax.ShapeDtypeStruct((depth, 8, n6), f32),
        grid=(depth, n6 // tn),
        in_specs=[
            pl.BlockSpec((8, d), lambda i, j: (0, 0)),
            pl.BlockSpec((1, d, tn), lambda i, j: (i, 0, j)),
            pl.BlockSpec((1, 1, tn), lambda i, j: (i, 0, j)),
        ],
        out_specs=pl.BlockSpec((1, 8, tn), lambda i, j: (i, 0, j)),
        compiler_params=_cparams(("arbitrary", "arbitrary")),
        name="modulation",
    )(cond, ada_w, ada_b.reshape(depth, 1, n6))


def _norm_mm_body(x_ref, sh_ref, sc_ref, w_ref, b_ref, o_ref, *, act):
    hn = _rms(x_ref[0]) * (1.0 + sc_ref[0]) + sh_ref[0]
    y = jnp.dot(hn.astype(bf16), w_ref[...], preferred_element_type=f32) + b_ref[...]
    if act == "gelu":
        y = 0.5 * y * (1.0 + lax.erf(y * (2.0 ** -0.5)))
    o_ref[0] = y.astype(o_ref.dtype)


def _norm_matmul(x, shift, scale, w, bias, act, out_dtype, tn):
    bsz, n, d = x.shape
    nn = w.shape[1]
    tm = min(512, n)
    return pl.pallas_call(
        functools.partial(_norm_mm_body, act=act),
        out_shape=jax.ShapeDtypeStruct((bsz, n, nn), out_dtype),
        grid=(nn // tn, bsz, n // tm),
        in_specs=[
            pl.BlockSpec((1, tm, d), lambda j, b, i: (b, i, 0)),
            pl.BlockSpec((1, 1, d), lambda j, b, i: (b, 0, 0)),
            pl.BlockSpec((1, 1, d), lambda j, b, i: (b, 0, 0)),
            pl.BlockSpec((d, tn), lambda j, b, i: (0, j)),
            pl.BlockSpec((1, tn), lambda j, b, i: (0, j)),
        ],
        out_specs=pl.BlockSpec((1, tm, tn), lambda j, b, i: (b, i, j)),
        compiler_params=_cparams(("arbitrary", "arbitrary", "arbitrary")),
        name="norm_matmul",
    )(x, shift, scale, w, bias)


def _mm_res_body(a_ref, w_ref, b_ref, x_ref, g_ref, o_ref):
    y = jnp.dot(a_ref[0].astype(bf16), w_ref[...], preferred_element_type=f32) + b_ref[...]
    o_ref[0] = x_ref[0] + g_ref[0] * y


def _matmul_residual(a, w, bias, x, gate):
    bsz, n, k = a.shape
    d = w.shape[1]
    tm = min(512, n)
    return pl.pallas_call(
        _mm_res_body,
        out_shape=jax.ShapeDtypeStruct((bsz, n, d), f32),
        grid=(bsz, n // tm),
        in_specs=[
            pl.BlockSpec((1, tm, k), lambda b, i: (b, i, 0)),
            pl.BlockSpec((k, d), lambda b, i: (0, 0)),
            pl.BlockSpec((1, d), lambda b, i: (0, 0)),
            pl.BlockSpec((1, tm, d), lambda b, i: (b, i, 0)),
            pl.BlockSpec((1, 1, d), lambda b, i: (b, 0, 0)),
        ],
        out_specs=pl.BlockSpec((1, tm, d), lambda b, i: (b, i, 0)),
        compiler_params=_cparams(("arbitrary", "arbitrary")),
        name="matmul_residual",
    )(a, w, bias, x, gate)


def _sg_body(u_ref, v_ref, vg_ref, ws_ref, bs_ref, wo_ref, bo_ref, x_ref, g_ref, o_ref, gated_ref):
    tm = u_ref.shape[1]
    gw = u_ref.shape[2] // SG_GROUPS
    v = (_rms(v_ref[0].astype(f32)) * vg_ref[...]).astype(bf16)
    for ch in range(tm // CHUNK):
        rows = slice(ch * CHUNK, (ch + 1) * CHUNK)
        for g in range(SG_GROUPS):
            cols = slice(g * gw, (g + 1) * gw)
            s = jnp.dot(ws_ref[g], v[rows, cols], preferred_element_type=f32) + bs_ref[g]
            gated_ref[rows, cols] = (u_ref[0, rows, cols].astype(f32) * s).astype(bf16)
    y = jnp.dot(gated_ref[...], wo_ref[...], preferred_element_type=f32) + bo_ref[...]
    o_ref[0] = x_ref[0] + g_ref[0] * y


def _sg_gate_out(zz, v_gain, w_s, b_s, w_out, b_out, x, gate):
    bsz, n, w2 = zz.shape
    wd = w2 // 2
    d = x.shape[-1]
    tm = min(256, n)
    return pl.pallas_call(
        _sg_body,
        out_shape=jax.ShapeDtypeStruct((bsz, n, d), f32),
        grid=(bsz, n // tm),
        in_specs=[
            pl.BlockSpec((1, tm, wd), lambda b, i: (b, i, 0)),
            pl.BlockSpec((1, tm, wd), lambda b, i: (b, i, 1)),
            pl.BlockSpec((1, wd), lambda b, i: (0, 0)),
            pl.BlockSpec((SG_GROUPS, CHUNK, CHUNK), lambda b, i: (0, 0, 0)),
            pl.BlockSpec((SG_GROUPS, CHUNK, 1), lambda b, i: (0, 0, 0)),
            pl.BlockSpec((wd, d), lambda b, i: (0, 0)),
            pl.BlockSpec((1, d), lambda b, i: (0, 0)),
            pl.BlockSpec((1, tm, d), lambda b, i: (b, i, 0)),
            pl.BlockSpec((1, 1, d), lambda b, i: (b, 0, 0)),
        ],
        out_specs=pl.BlockSpec((1, tm, d), lambda b, i: (b, i, 0)),
        scratch_shapes=[pltpu.VMEM((tm, wd), bf16)],
        compiler_params=_cparams(("arbitrary", "arbitrary")),
        name="sg_gate_out",
    )(zz, zz, v_gain, w_s, b_s, w_out, b_out, x, gate)


def _sg_mixer(x, sh, sc, gate, w_in, b_in, v_gain, w_s, b_s, w_out, b_out):
    zz = _norm_matmul(x, sh, sc, w_in.astype(bf16), b_in[None], "gelu", bf16, 2048)
    return _sg_gate_out(zz, v_gain[None], w_s.astype(bf16), b_s[:, :, None], w_out.astype(bf16), b_out[None], x, gate)


def _qk_prep_body(q_ref, k_ref, v_ref, cs_ref, qg_ref, kg_ref, bd_ref, qo_ref, ko_ref, vo_ref, *, rope, dh):
    tm, dq = q_ref.shape[1], q_ref.shape[2]
    reps = dq // 128

    def group_ms(x):
        x2 = x * x
        hi = x2.astype(bf16)
        lo = (x2 - hi.astype(f32)).astype(bf16)
        cols = []
        for j in range(reps):
            sl = slice(j * 128, (j + 1) * 128)
            cols.append(jnp.dot(hi[:, sl], bd_ref[...], preferred_element_type=f32)
                        + jnp.dot(lo[:, sl], bd_ref[...], preferred_element_type=f32))
        return jnp.concatenate(cols, axis=1) * (1.0 / dh)

    def prep(x, gain):
        y = x * lax.rsqrt(group_ms(x) + EPS) * gain
        if rope:
            cos = jnp.concatenate([cs_ref[0]] * reps, axis=1)
            sin = jnp.concatenate([cs_ref[1]] * reps, axis=1)
            lane = lax.broadcasted_iota(jnp.int32, y.shape, 1)
            quarter = dh // 4
            partner = jnp.where((lane & quarter) != 0, pltpu.roll(y, quarter, axis=1), pltpu.roll(y, dq - quarter, axis=1))
            y = y * cos + partner * sin
        return y

    qo_ref[0] = (prep(q_ref[0], qg_ref[...]) * (dh ** -0.5)).astype(bf16)
    ko_ref[0] = prep(k_ref[0], kg_ref[...]).astype(bf16)
    vo_ref[0] = v_ref[0].astype(bf16)


def _qk_prep(qkv, cs, q_gain, k_gain, rope):
    bsz, n, d3 = qkv.shape
    dq = d3 // 3
    dh = q_gain.shape[0]
    tm = min(512, n)
    lane = jnp.arange(128)
    bd = (lane[:, None] // dh == lane[None, :] // dh).astype(bf16)
    qg = jnp.tile(q_gain, dq // dh)[None]
    kg = jnp.tile(k_gain, dq // dh)[None]
    out = jax.ShapeDtypeStruct((bsz, n, dq), bf16)
    return pl.pallas_call(
        functools.partial(_qk_prep_body, rope=rope, dh=dh),
        out_shape=(out, out, out),
        grid=(bsz, n // tm),
        in_specs=[
            pl.BlockSpec((1, tm, dq), lambda b, i: (b, i, 0)),
            pl.BlockSpec((1, tm, dq), lambda b, i: (b, i, 1)),
            pl.BlockSpec((1, tm, dq), lambda b, i: (b, i, 2)),
            pl.BlockSpec((2, tm, 128), lambda b, i: (0, i, 0)),
            pl.BlockSpec((1, dq), lambda b, i: (0, 0)),
            pl.BlockSpec((1, dq), lambda b, i: (0, 0)),
            pl.BlockSpec((128, 128), lambda b, i: (0, 0)),
        ],
        out_specs=tuple(pl.BlockSpec((1, tm, dq), lambda b, i: (b, i, 0)) for _ in range(3)),
        compiler_params=_cparams(("arbitrary", "arbitrary")),
        name="qk_prep",
    )(qkv, qkv, qkv, cs, qg, kg, bd)


def _attn_body(lam_ref, q_ref, k_ref, v_ref, sg_ref, o_ref, *, tk, lam_init):
    tq, dv = q_ref.shape[1], q_ref.shape[2]
    nk = k_ref.shape[1]
    dh = dv // 2
    lq1, lk1, lq2, lk2 = lam_ref[0:1], lam_ref[1:2], lam_ref[2:3], lam_ref[3:4]
    lam = jnp.exp(jnp.sum(lq1 * lk1, keepdims=True)) - jnp.exp(jnp.sum(lq2 * lk2, keepdims=True)) + lam_init
    q = q_ref[0]
    lane = lax.broadcasted_iota(jnp.int32, q.shape, 1)
    zero = jnp.zeros_like(q)
    q0 = jnp.where(lane < dh, q, zero)
    q1 = jnp.where(lane >= dh, q, zero)
    qq = jnp.concatenate([q0, q1], axis=0)

    def step(j, carry):
        m, l, acc = carry
        kb = k_ref[0, pl.ds(j * tk, tk), :]
        vb = v_ref[0, pl.ds(j * tk, tk), :]
        s = lax.dot_general(qq, kb, (((1,), (1,)), ((), ())), preferred_element_type=f32)
        m_new = jnp.maximum(m, jnp.max(s, axis=-1, keepdims=True))
        alpha = jnp.exp(m - m_new)
        p = jnp.exp(s - m_new)
        l = alpha * l + jnp.sum(p, axis=-1, keepdims=True)
        acc = alpha * acc + jnp.dot(p.astype(bf16), vb, preferred_element_type=f32)
        return m_new, l, acc

    init = (jnp.full((2 * tq, 1), -jnp.inf, f32), jnp.zeros((2 * tq, 1), f32), jnp.zeros((2 * tq, dv), f32))
    m, l, acc = lax.fori_loop(0, nk // tk, step, init)
    o = acc / l
    o = o[:tq] - lam * o[tq:]
    o_ref[0] = (_rms(o) * sg_ref[...]).astype(o_ref.dtype)


def _diff_attention(q, k, v, lam_vecs, subln, lam_init):
    bsz, n, dq = q.shape
    nk = k.shape[1]
    dv = dq // DA_HEADS
    tq = 256
    tk = 768
    assert nk % tk == 0
    return pl.pallas_call(
        functools.partial(_attn_body, tk=tk, lam_init=lam_init),
        out_shape=jax.ShapeDtypeStruct((bsz, n, dq), bf16),
        grid=(bsz, DA_HEADS, n // tq),
        in_specs=[
            pl.BlockSpec((4, dv // 2), lambda b, h, i: (0, 0)),
            pl.BlockSpec((1, tq, dv), lambda b, h, i: (b, i, h)),
            pl.BlockSpec((1, nk, dv), lambda b, h, i: (b, 0, h)),
            pl.BlockSpec((1, nk, dv), lambda b, h, i: (b, 0, h)),
            pl.BlockSpec((1, dv), lambda b, h, i: (0, 0)),
        ],
        out_specs=pl.BlockSpec((1, tq, dv), lambda b, h, i: (b, i, h)),
        compiler_params=_cparams(("arbitrary", "arbitrary", "arbitrary")),
        name="diff_attention",
    )(lam_vecs, q, k, v, subln)


def _rope_tables(n, dh):
    quarter = dh // 4
    t = jnp.arange(n)
    row = (t // GRID_W).astype(f32)
    col = (t % GRID_W).astype(f32)
    inv = ROPE_BASE ** (-jnp.arange(quarter, dtype=f32) / quarter)
    j = jnp.arange(128) % dh
    ax, half, fr = j // (2 * quarter), (j % (2 * quarter)) // quarter, j % quarter
    ang = jnp.where(ax[None, :] == 0, row[:, None], col[:, None]) * inv[fr][None, :]
    sign = jnp.where(half == 0, -1.0, 1.0)[None, :]
    return jnp.stack([jnp.cos(ang), jnp.sin(ang) * sign])


def _attn_mixer(x, sh, sc, gate, ctx, csh, csc, layer_idx, w_in, q_gain, k_gain, lq1, lk1, lq2, lk2, subln, w_out):
    bsz, n, d = x.shape
    dh = q_gain.shape[0]
    w = w_in.astype(bf16)
    zero_b = jnp.zeros((1, w.shape[1]), f32)
    lam_init = 0.8 - 0.6 * math.exp(-0.3 * layer_idx)
    qkv_x = _norm_matmul(x, sh, sc, w, zero_b, None, f32, 1024)
    qkv_c = _norm_matmul(ctx, csh, csc, w, zero_b, None, f32, 1024)
    cs = _rope_tables(n, dh)
    qx, kx, vx = _qk_prep(qkv_x, cs, q_gain, k_gain, True)
    _, kc, vc = _qk_prep(qkv_c, cs, q_gain, k_gain, False)
    k_all = jnp.concatenate([kc, kx], axis=1)
    v_all = jnp.concatenate([vc, vx], axis=1)
    o = _diff_attention(qx, k_all, v_all, jnp.stack([lq1, lk1, lq2, lk2]), (subln * (1.0 - lam_init))[None], lam_init)
    return _matmul_residual(o, w_out.astype(bf16), jnp.zeros((1, d), f32), x, gate)


def _moe_pre_body(x_ref, sh_ref, sc_ref, wr_ref, h_ref, a_ref):
    hn = _rms(x_ref[0]) * (1.0 + sc_ref[0]) + sh_ref[0]
    h_ref[0] = hn.astype(h_ref.dtype)
    lg = lax.dot_general(wr_ref[...], hn, (((1,), (1,)), ((), ())), precision=HIGHEST, preferred_element_type=f32)
    e = jnp.exp(lg - jnp.max(lg, axis=0, keepdims=True))
    a_ref[0] = e / jnp.sum(e, axis=0, keepdims=True)


def _moe_pre(x, shift, scale, w_router_t):
    bsz, n, d = x.shape
    ne = w_router_t.shape[0]
    tm = min(512, n)
    return pl.pallas_call(
        _moe_pre_body,
        out_shape=(jax.ShapeDtypeStruct((bsz, n, d), bf16), jax.ShapeDtypeStruct((bsz, ne, n), f32)),
        grid=(bsz, n // tm),
        in_specs=[
            pl.BlockSpec((1, tm, d), lambda b, i: (b, i, 0)),
            pl.BlockSpec((1, 1, d), lambda b, i: (b, 0, 0)),
            pl.BlockSpec((1, 1, d), lambda b, i: (b, 0, 0)),
            pl.BlockSpec((ne, d), lambda b, i: (0, 0)),
        ],
        out_specs=(pl.BlockSpec((1, tm, d), lambda b, i: (b, i, 0)), pl.BlockSpec((1, ne, tm), lambda b, i: (b, 0, i))),
        compiler_params=_cparams(("arbitrary", "arbitrary")),
        name="moe_pre",
    )(x, shift, scale, w_router_t)


def _expert_body(xs_ref, g_ref, wg_ref, wu_ref, wd_ref, o_ref):
    f = pl.program_id(1)
    bsz, _, cap, d = xs_ref.shape
    xb = xs_ref[...].reshape(bsz * cap, d)
    a = jnp.dot(xb, wg_ref[0].astype(bf16), preferred_element_type=f32)
    b = jnp.dot(xb, wu_ref[0].astype(bf16), preferred_element_type=f32)
    hmid = (a * (1.0 / (1.0 + jnp.exp(-a))) * b).astype(bf16)
    part = jnp.dot(hmid, wd_ref[0].astype(bf16), preferred_element_type=f32).reshape(o_ref.shape)

    @pl.when(f == 0)
    def _():
        o_ref[...] = part

    @pl.when(f > 0)
    def _():
        o_ref[...] += part

    @pl.when(f == pl.num_programs(1) - 1)
    def _():
        o_ref[...] *= g_ref[...]


def _expert_ffn(xs, g, w_gate, w_up, w_down):
    bsz, ne, cap, d = xs.shape
    ff = w_gate.shape[2]
    tf = 512
    return pl.pallas_call(
        _expert_body,
        out_shape=jax.ShapeDtypeStruct((bsz, ne, cap, d), f32),
        grid=(ne, ff // tf),
        in_specs=[
            pl.BlockSpec((bsz, 1, cap, d), lambda e, f: (0, e, 0, 0)),
            pl.BlockSpec((bsz, 1, cap, 1), lambda e, f: (0, e, 0, 0)),
            pl.BlockSpec((1, d, tf), lambda e, f: (e, 0, f)),
            pl.BlockSpec((1, d, tf), lambda e, f: (e, 0, f)),
            pl.BlockSpec((1, tf, d), lambda e, f: (e, f, 0)),
        ],
        out_specs=pl.BlockSpec((bsz, 1, cap, d), lambda e, f: (0, e, 0, 0)),
        compiler_params=_cparams(("arbitrary", "arbitrary")),
        name="expert_ffn",
    )(xs, g, w_gate, w_up, w_down)


def _moe(x, shift, scale, gate, w_router, w_gate, w_up, w_down):
    bsz, n, d = x.shape
    ne = w_router.shape[1]
    cap = EC_CAPACITY * n // ne
    hn, aff = _moe_pre(x, shift, scale, w_router.T)
    g, idx = lax.top_k(aff, cap)
    xs = jax.vmap(lambda hb, ib: hb[ib])(hn, idx)
    ye = _expert_ffn(xs, g[..., None], w_gate, w_up, w_down)
    y = jax.vmap(lambda yb, ib: jnp.zeros((n, d), f32).at[ib.reshape(-1)].add(yb.reshape(-1, d)))(ye, idx)
    return x + gate * y


def _hyena_filter_spectra(n, w1, b1, w2, b2, w3, b3, freq, w_out):
    d = w_out.shape[1] // (HY_ORDER * 2)
    t = jnp.linspace(0.0, 1.0, n, dtype=f32)[:, None]
    bands = (HY_POS_EMB - 1) // 2
    f = jnp.linspace(1e-4, bands - 1, bands, dtype=f32)
    w = (2.0 * math.pi / n) * jnp.arange(n, dtype=f32)[:, None]
    z = jnp.concatenate([t, jnp.cos(f * w), -jnp.sin(f * w)], axis=-1)
    hid = jnp.sin(freq * (jnp.dot(z, w1, precision=HIGHEST) + b1))
    hid = jnp.sin(freq * (jnp.dot(hid, w2, precision=HIGHEST) + b2))
    hid = jnp.sin(freq * (jnp.dot(hid, w3, precision=HIGHEST) + b3))
    h = jnp.dot(hid, w_out, precision=HIGHEST).reshape(n, HY_ORDER, 2, d)
    deltas = jnp.abs(jnp.linspace(HY_MIN_DECAY, HY_MAX_DECAY, d, dtype=f32))
    h = h * jnp.exp(-t * deltas)[:, None, None, :]
    full = jnp.concatenate([h[:, :, 0], jnp.zeros((1, HY_ORDER, d), f32), h[:0:-1, :, 1]], axis=0)
    full = full * lax.rsqrt(jnp.sum(full * full, axis=0, keepdims=True) + EPS)
    return jnp.fft.rfft(full, axis=0)


def _hyena_mixer(x, sh, sc, gate, w_in, b_in, conv_w, conv_b, f_w1, f_b1, f_w2, f_b2, f_w3, f_b3, f_freq, f_wout,
                 skip, w_out, b_out):
    bsz, n, d = x.shape
    spec = _hyena_filter_spectra(n, f_w1, f_b1, f_w2, f_b2, f_w3, f_b3, f_freq, f_wout)
    u0 = _norm_matmul(x, sh, sc, w_in.astype(bf16), b_in[None], None, f32, 1024)
    pad = HY_SHORT // 2
    up = jnp.pad(u0, ((0, 0), (pad, pad), (0, 0)))
    u = up[:, 0:n] * conv_w[0]
    for k in range(1, HY_SHORT):
        u = u + up[:, k:k + n] * conv_w[k]
    u = u + conv_b
    v, x1, x2 = jnp.split(u, 3, axis=-1)
    z = v
    for o, gt in enumerate((x1, x2)):
        y = jnp.fft.irfft(jnp.fft.rfft(z, n=2 * n, axis=1) * spec[:, o], n=2 * n, axis=1)[:, :n]
        z = gt * (y + skip[o] * z)
    return _matmul_residual(z, w_out.astype(bf16), b_out[None], x, gate)


def kernel(x, c, ctx, c_ctx, ada_w, ada_b, sg_w_in, sg_b_in, sg_v_gain, sg_w_s, sg_b_s, sg_w_out, sg_b_out, hy_w_in, hy_b_in, hy_conv_w, hy_conv_b, hy_f_w1, hy_f_b1, hy_f_w2, hy_f_b2, hy_f_w3, hy_f_b3, hy_f_freq, hy_f_wout, hy_skip, hy_w_out, hy_b_out, da_w_in, da_q_gain, da_k_gain, da_lam_q1, da_lam_k1, da_lam_q2, da_lam_k2, da_subln_gain, da_w_out, moe_router, moe_w_gate, moe_w_up, moe_w_down):
    bsz, n, d = x.shape
    depth = ada_w.shape[0]
    cond = jnp.zeros((8, d), f32).at[:bsz].set(c).at[bsz].set(c_ctx)
    mods = _modulation(cond, ada_w, ada_b)

    for i in range(depth):
        kind, j = i % N_MIXERS, i // N_MIXERS
        ctx_live = any(l % N_MIXERS == 2 for l in range(i + 1, depth))
        assert not (kind == 2 and ctx_live), "context-query attention is not needed at this depth and not implemented"
        mx = [mods[i, :bsz, k * d:(k + 1) * d][:, None, :] for k in range(6)]
        mc = [jnp.broadcast_to(mods[i, bsz, k * d:(k + 1) * d], (bsz, 1, d)) for k in range(6)]
        streams = [(x, mx)] + ([(ctx, mc)] if ctx_live else [])
        outs = []
        for h, m in streams:
            sh1, sc1, g1, sh2, sc2, g2 = m
            if kind == 0:
                h = _sg_mixer(h, sh1, sc1, g1, sg_w_in[j], sg_b_in[j], sg_v_gain[j], sg_w_s[j], sg_b_s[j],
                              sg_w_out[j], sg_b_out[j])
            elif kind == 1:
                h = _hyena_mixer(h, sh1, sc1, g1, hy_w_in[j], hy_b_in[j], hy_conv_w[j], hy_conv_b[j], hy_f_w1[j],
                                 hy_f_b1[j], hy_f_w2[j], hy_f_b2[j], hy_f_w3[j], hy_f_b3[j], hy_f_freq[j],
                                 hy_f_wout[j], hy_skip[j], hy_w_out[j], hy_b_out[j])
            else:
                h = _attn_mixer(h, sh1, sc1, g1, ctx, mc[0], mc[1], i, da_w_in[j], da_q_gain[j], da_k_gain[j],
                                da_lam_q1[j], da_lam_k1[j], da_lam_q2[j], da_lam_k2[j], da_subln_gain[j], da_w_out[j])
            h = _moe(h, sh2, sc2, g2, moe_router[i], moe_w_gate[i], moe_w_up[i], moe_w_down[i])
            outs.append(h)
        x = outs[0]
        if ctx_live:
            ctx = outs[1]
    return x
```

```python
import functools
import math

import jax
import jax.numpy as jnp
import numpy as np
from jax import lax
from jax.experimental import pallas as pl
from jax.experimental.pallas import tpu as pltpu

f32 = jnp.float32
bf16 = jnp.bfloat16
HIGHEST = lax.Precision.HIGHEST

EPS = 1e-6
N_MIXERS = 3
CHUNK = 128
SG_GROUPS = 8
HY_ORDER = 2
HY_SHORT = 3
HY_POS_EMB = 33
HY_TARGET = 1e-2
HY_FAST_DECAY_PCT = 0.3
HY_SLOW_DECAY_PCT = 1.5
HY_MAX_DECAY = math.log(HY_TARGET) / HY_FAST_DECAY_PCT
HY_MIN_DECAY = math.log(HY_TARGET) / HY_SLOW_DECAY_PCT
GRID_W = 64
DA_HEADS = 8
ROPE_BASE = 10000.0
EC_CAPACITY = 2

VMEM_LIMIT = 56 * 1024 * 1024


def _cparams(sem):
    return pltpu.CompilerParams(dimension_semantics=sem, vmem_limit_bytes=VMEM_LIMIT)


def _rms(x):
    return x * lax.rsqrt(jnp.mean(x * x, axis=-1, keepdims=True) + EPS)


def _mod_body(c_ref, w_ref, b_ref, o_ref):
    c = c_ref[...]
    s = c * (1.0 / (1.0 + jnp.exp(-c)))
    o_ref[0] = jnp.dot(s, w_ref[0], precision=HIGHEST, preferred_element_type=f32) + b_ref[0]


def _modulation(cond, ada_w, ada_b):
    depth, d, n6 = ada_w.shape
    tn = 1536
    return pl.pallas_call(
        _mod_body,
        out_shape=jax.ShapeDtypeStruct((depth, 8, n6), f32),
        grid=(depth, n6 // tn),
        in_specs=[
            pl.BlockSpec((8, d), lambda i, j: (0, 0)),
            pl.BlockSpec((1, d, tn), lambda i, j: (i, 0, j)),
            pl.BlockSpec((1, 1, tn), lambda i, j: (i, 0, j)),
        ],
        out_specs=pl.BlockSpec((1, 8, tn), lambda i, j: (i, 0, j)),
        compiler_params=_cparams(("arbitrary", "arbitrary")),
        name="modulation",
    )(cond, ada_w, ada_b.reshape(depth, 1, n6))


def _norm_mm_body(x_ref, sh_ref, sc_ref, w_ref, b_ref, o_ref, *, act):
    hn = _rms(x_ref[0]) * (1.0 + sc_ref[0]) + sh_ref[0]
    y = jnp.dot(hn.astype(bf16), w_ref[...], preferred_element_type=f32) + b_ref[...]
    if act == "gelu":
        y = 0.5 * y * (1.0 + lax.erf(y * (2.0 ** -0.5)))
    o_ref[0] = y.astype(o_ref.dtype)


def _norm_matmul(x, shift, scale, w, bias, act, out_dtype, tn):
    bsz, n, d = x.shape
    nn = w.shape[1]
    tm = min(512, n)
    return pl.pallas_call(
        functools.partial(_norm_mm_body, act=act),
        out_shape=jax.ShapeDtypeStruct((bsz, n, nn), out_dtype),
        grid=(nn // tn, bsz, n // tm),
        in_specs=[
            pl.BlockSpec((1, tm, d), lambda j, b, i: (b, i, 0)),
            pl.BlockSpec((1, 1, d), lambda j, b, i: (b, 0, 0)),
            pl.BlockSpec((1, 1, d), lambda j, b, i: (b, 0, 0)),
            pl.BlockSpec((d, tn), lambda j, b, i: (0, j)),
            pl.BlockSpec((1, tn), lambda j, b, i: (0, j)),
        ],
        out_specs=pl.BlockSpec((1, tm, tn), lambda j, b, i: (b, i, j)),
        compiler_params=_cparams(("arbitrary", "arbitrary", "arbitrary")),
        name="norm_matmul",
    )(x, shift, scale, w, bias)


def _mm_res_body(a_ref, w_ref, b_ref, x_ref, g_ref, o_ref):
    y = jnp.dot(a_ref[0].astype(bf16), w_ref[...], preferred_element_type=f32) + b_ref[...]
    o_ref[0] = x_ref[0] + g_ref[0] * y


def _matmul_residual(a, w, bias, x, gate):
    bsz, n, k = a.shape
    d = w.shape[1]
    tm = min(512, n)
    return pl.pallas_call(
        _mm_res_body,
        out_shape=jax.ShapeDtypeStruct((bsz, n, d), f32),
        grid=(bsz, n // tm),
        in_specs=[
            pl.BlockSpec((1, tm, k), lambda b, i: (b, i, 0)),
            pl.BlockSpec((k, d), lambda b, i: (0, 0)),
            pl.BlockSpec((1, d), lambda b, i: (0, 0)),
            pl.BlockSpec((1, tm, d), lambda b, i: (b, i, 0)),
            pl.BlockSpec((1, 1, d), lambda b, i: (b, 0, 0)),
        ],
        out_specs=pl.BlockSpec((1, tm, d), lambda b, i: (b, i, 0)),
        compiler_params=_cparams(("arbitrary", "arbitrary")),
        name="matmul_residual",
    )(a, w, bias, x, gate)


def _sg_body(u_ref, v_ref, vg_ref, ws_ref, bs_ref, wo_ref, bo_ref, x_ref, g_ref, o_ref, gated_ref):
    tm = u_ref.shape[1]
    gw = u_ref.shape[2] // SG_GROUPS
    v = (_rms(v_ref[0].astype(f32)) * vg_ref[...]).astype(bf16)
    for ch in range(tm // CHUNK):
        rows = slice(ch * CHUNK, (ch + 1) * CHUNK)
        for g in range(SG_GROUPS):
            cols = slice(g * gw, (g + 1) * gw)
            s = jnp.dot(ws_ref[g], v[rows, cols], preferred_element_type=f32) + bs_ref[g]
            gated_ref[rows, cols] = (u_ref[0, rows, cols].astype(f32) * s).astype(bf16)
    y = jnp.dot(gated_ref[...], wo_ref[...], preferred_element_type=f32) + bo_ref[...]
    o_ref[0] = x_ref[0] + g_ref[0] * y


def _sg_gate_out(zz, v_gain, w_s, b_s, w_out, b_out, x, gate):
    bsz, n, w2 = zz.shape
    wd = w2 // 2
    d = x.shape[-1]
    tm = min(256, n)
    return pl.pallas_call(
        _sg_body,
        out_shape=jax.ShapeDtypeStruct((bsz, n, d), f32),
        grid=(bsz, n // tm),
        in_specs=[
            pl.BlockSpec((1, tm, wd), lambda b, i: (b, i, 0)),
            pl.BlockSpec((1, tm, wd), lambda b, i: (b, i, 1)),
            pl.BlockSpec((1, wd), lambda b, i: (0, 0)),
            pl.BlockSpec((SG_GROUPS, CHUNK, CHUNK), lambda b, i: (0, 0, 0)),
            pl.BlockSpec((SG_GROUPS, CHUNK, 1), lambda b, i: (0, 0, 0)),
            pl.BlockSpec((wd, d), lambda b, i: (0, 0)),
            pl.BlockSpec((1, d), lambda b, i: (0, 0)),
            pl.BlockSpec((1, tm, d), lambda b, i: (b, i, 0)),
            pl.BlockSpec((1, 1, d), lambda b, i: (b, 0, 0)),
        ],
        out_specs=pl.BlockSpec((1, tm, d), lambda b, i: (b, i, 0)),
        scratch_shapes=[pltpu.VMEM((tm, wd), bf16)],
        compiler_params=_cparams(("arbitrary", "arbitrary")),
        name="sg_gate_out",
    )(zz, zz, v_gain, w_s, b_s, w_out, b_out, x, gate)


def _sg_mixer(x, sh, sc, gate, w_in, b_in, v_gain, w_s, b_s, w_out, b_out):
    zz = _norm_matmul(x, sh, sc, w_in.astype(bf16), b_in[None], "gelu", bf16, 2048)
    return _sg_gate_out(zz, v_gain[None], w_s.astype(bf16), b_s[:, :, None], w_out.astype(bf16), b_out[None], x, gate)


def _qk_prep_body(q_ref, k_ref, v_ref, cs_ref, qg_ref, kg_ref, bd_ref, qo_ref, ko_ref, vo_ref, *, rope, dh):
    tm, dq = q_ref.shape[1], q_ref.shape[2]
    reps = dq // 128

    def group_ms(x):
        x2 = x * x
        hi = x2.astype(bf16)
        lo = (x2 - hi.astype(f32)).astype(bf16)
        cols = []
        for j in range(reps):
            sl = slice(j * 128, (j + 1) * 128)
            cols.append(jnp.dot(hi[:, sl], bd_ref[...], preferred_element_type=f32)
                        + jnp.dot(lo[:, sl], bd_ref[...], preferred_element_type=f32))
        return jnp.concatenate(cols, axis=1) * (1.0 / dh)

    def prep(x, gain):
        y = x * lax.rsqrt(group_ms(x) + EPS) * gain
        if rope:
            cos = jnp.concatenate([cs_ref[0]] * reps, axis=1)
            sin = jnp.concatenate([cs_ref[1]] * reps, axis=1)
            lane = lax.broadcasted_iota(jnp.int32, y.shape, 1)
            quarter = dh // 4
            partner = jnp.where((lane & quarter) != 0, pltpu.roll(y, quarter, axis=1), pltpu.roll(y, dq - quarter, axis=1))
            y = y * cos + partner * sin
        return y

    qo_ref[0] = (prep(q_ref[0], qg_ref[...]) * (dh ** -0.5)).astype(bf16)
    ko_ref[0] = prep(k_ref[0], kg_ref[...]).astype(bf16)
    vo_ref[0] = v_ref[0].astype(bf16)


def _qk_prep(qkv, cs, q_gain, k_gain, rope):
    bsz, n, d3 = qkv.shape
    dq = d3 // 3
    dh = q_gain.shape[0]
    tm = min(512, n)
    lane = jnp.arange(128)
    bd = (lane[:, None] // dh == lane[None, :] // dh).astype(bf16)
    qg = jnp.tile(q_gain, dq // dh)[None]
    kg = jnp.tile(k_gain, dq // dh)[None]
    out = jax.ShapeDtypeStruct((bsz, n, dq), bf16)
    return pl.pallas_call(
        functools.partial(_qk_prep_body, rope=rope, dh=dh),
        out_shape=(out, out, out),
        grid=(bsz, n // tm),
        in_specs=[
            pl.BlockSpec((1, tm, dq), lambda b, i: (b, i, 0)),
            pl.BlockSpec((1, tm, dq), lambda b, i: (b, i, 1)),
            pl.BlockSpec((1, tm, dq), lambda b, i: (b, i, 2)),
            pl.BlockSpec((2, tm, 128), lambda b, i: (0, i, 0)),
            pl.BlockSpec((1, dq), lambda b, i: (0, 0)),
            pl.BlockSpec((1, dq), lambda b, i: (0, 0)),
            pl.BlockSpec((128, 128), lambda b, i: (0, 0)),
        ],
        out_specs=tuple(pl.BlockSpec((1, tm, dq), lambda b, i: (b, i, 0)) for _ in range(3)),
        compiler_params=_cparams(("arbitrary", "arbitrary")),
        name="qk_prep",
    )(qkv, qkv, qkv, cs, qg, kg, bd)


def _attn_body(lam_ref, q_ref, k_ref, v_ref, sg_ref, o_ref, *, tk, lam_init):
    tq, dv = q_ref.shape[1], q_ref.shape[2]
    nk = k_ref.shape[1]
    dh = dv // 2
    lq1, lk1, lq2, lk2 = lam_ref[0:1], lam_ref[1:2], lam_ref[2:3], lam_ref[3:4]
    lam = jnp.exp(jnp.sum(lq1 * lk1, keepdims=True)) - jnp.exp(jnp.sum(lq2 * lk2, keepdims=True)) + lam_init
    q = q_ref[0]
    lane = lax.broadcasted_iota(jnp.int32, q.shape, 1)
    zero = jnp.zeros_like(q)
    q0 = jnp.where(lane < dh, q, zero)
    q1 = jnp.where(lane >= dh, q, zero)
    qq = jnp.concatenate([q0, q1], axis=0)

    def step(j, carry):
        m, l, acc = carry
        kb = k_ref[0, pl.ds(j * tk, tk), :]
        vb = v_ref[0, pl.ds(j * tk, tk), :]
        s = lax.dot_general(qq, kb, (((1,), (1,)), ((), ())), preferred_element_type=f32)
        m_new = jnp.maximum(m, jnp.max(s, axis=-1, keepdims=True))
        alpha = jnp.exp(m - m_new)
        p = jnp.exp(s - m_new)
        l = alpha * l + jnp.sum(p, axis=-1, keepdims=True)
        acc = alpha * acc + jnp.dot(p.astype(bf16), vb, preferred_element_type=f32)
        return m_new, l, acc

    init = (jnp.full((2 * tq, 1), -jnp.inf, f32), jnp.zeros((2 * tq, 1), f32), jnp.zeros((2 * tq, dv), f32))
    m, l, acc = lax.fori_loop(0, nk // tk, step, init)
    o = acc / l
    o = o[:tq] - lam * o[tq:]
    o_ref[0] = (_rms(o) * sg_ref[...]).astype(o_ref.dtype)


def _diff_attention(q, k, v, lam_vecs, subln, lam_init):
    bsz, n, dq = q.shape
    nk = k.shape[1]
    dv = dq // DA_HEADS
    tq = 256
    tk = 768
    assert nk % tk == 0
    return pl.pallas_call(
        functools.partial(_attn_body, tk=tk, lam_init=lam_init),
        out_shape=jax.ShapeDtypeStruct((bsz, n, dq), bf16),
        grid=(bsz, DA_HEADS, n // tq),
        in_specs=[
            pl.BlockSpec((4, dv // 2), lambda b, h, i: (0, 0)),
            pl.BlockSpec((1, tq, dv), lambda b, h, i: (b, i, h)),
            pl.BlockSpec((1, nk, dv), lambda b, h, i: (b, 0, h)),
            pl.BlockSpec((1, nk, dv), lambda b, h, i: (b, 0, h)),
            pl.BlockSpec((1, dv), lambda b, h, i: (0, 0)),
        ],
        out_specs=pl.BlockSpec((1, tq, dv), lambda b, h, i: (b, i, h)),
        compiler_params=_cparams(("arbitrary", "arbitrary", "arbitrary")),
        name="diff_attention",
    )(lam_vecs, q, k, v, subln)


def _rope_tables(n, dh):
    quarter = dh // 4
    t = jnp.arange(n)
    row = (t // GRID_W).astype(f32)
    col = (t % GRID_W).astype(f32)
    inv = ROPE_BASE ** (-jnp.arange(quarter, dtype=f32) / quarter)
    j = jnp.arange(128) % dh
    ax, half, fr = j // (2 * quarter), (j % (2 * quarter)) // quarter, j % quarter
    ang = jnp.where(ax[None, :] == 0, row[:, None], col[:, None]) * inv[fr][None, :]
    sign = jnp.where(half == 0, -1.0, 1.0)[None, :]
    return jnp.stack([jnp.cos(ang), jnp.sin(ang) * sign])


def _attn_mixer(x, sh, sc, gate, ctx, csh, csc, layer_idx, w_in, q_gain, k_gain, lq1, lk1, lq2, lk2, subln, w_out):
    bsz, n, d = x.shape
    dh = q_gain.shape[0]
    w = w_in.astype(bf16)
    zero_b = jnp.zeros((1, w.shape[1]), f32)
    lam_init = 0.8 - 0.6 * math.exp(-0.3 * layer_idx)
    qkv_x = _norm_matmul(x, sh, sc, w, zero_b, None, f32, 1024)
    qkv_c = _norm_matmul(ctx, csh, csc, w, zero_b, None, f32, 1024)
    cs = _rope_tables(n, dh)
    qx, kx, vx = _qk_prep(qkv_x, cs, q_gain, k_gain, True)
    _, kc, vc = _qk_prep(qkv_c, cs, q_gain, k_gain, False)
    k_all = jnp.concatenate([kc, kx], axis=1)
    v_all = jnp.concatenate([vc, vx], axis=1)
    o = _diff_attention(qx, k_all, v_all, jnp.stack([lq1, lk1, lq2, lk2]), (subln * (1.0 - lam_init))[None], lam_init)
    return _matmul_residual(o, w_out.astype(bf16), jnp.zeros((1, d), f32), x, gate)


def _moe_pre_body(x_ref, sh_ref, sc_ref, wr_ref, h_ref, a_ref):
    hn = _rms(x_ref[0]) * (1.0 + sc_ref[0]) + sh_ref[0]
    h_ref[0] = hn.astype(h_ref.dtype)
    lg = lax.dot_general(wr_ref[...], hn, (((1,), (1,)), ((), ())), precision=HIGHEST, preferred_element_type=f32)
    e = jnp.exp(lg - jnp.max(lg, axis=0, keepdims=True))
    a_ref[0] = e / jnp.sum(e, axis=0, keepdims=True)


def _moe_pre(x, shift, scale, w_router_t):
    bsz, n, d = x.shape
    ne = w_router_t.shape[0]
    tm = min(512, n)
    return pl.pallas_call(
        _moe_pre_body,
        out_shape=(jax.ShapeDtypeStruct((bsz, n, d), f32), jax.ShapeDtypeStruct((bsz, ne, n), f32)),
        grid=(bsz, n // tm),
        in_specs=[
            pl.BlockSpec((1, tm, d), lambda b, i: (b, i, 0)),
            pl.BlockSpec((1, 1, d), lambda b, i: (b, 0, 0)),
            pl.BlockSpec((1, 1, d), lambda b, i: (b, 0, 0)),
            pl.BlockSpec((ne, d), lambda b, i: (0, 0)),
        ],
        out_specs=(pl.BlockSpec((1, tm, d), lambda b, i: (b, i, 0)), pl.BlockSpec((1, ne, tm), lambda b, i: (b, 0, i))),
        compiler_params=_cparams(("arbitrary", "arbitrary")),
        name="moe_pre",
    )(x, shift, scale, w_router_t)


LANES = 128
SUBLANES_BF16 = 16


def _route_body(a_ref, idx_ref, pos_ref, st_ref, pf_ref, cols_ref, *, cap):
    ne, n = a_ref.shape[1], a_ref.shape[2]
    nblk = n // LANES
    bits = pltpu.bitcast(a_ref[0], jnp.int32)

    def bisect(i, thr):
        cand = thr | lax.shift_left(jnp.int32(1), 30 - i)
        cnt = jnp.sum((bits >= cand).astype(f32), axis=1, keepdims=True)
        return jnp.where(cnt >= cap, cand, thr)

    thr = lax.fori_loop(0, 31, bisect, jnp.zeros((ne, 1), jnp.int32))
    gt, eq = bits > thr, bits == thr
    need = cap - jnp.sum(gt.astype(f32), axis=1, keepdims=True)
    r_i = lax.broadcasted_iota(jnp.int32, (LANES, LANES), 0)
    c_i = lax.broadcasted_iota(jnp.int32, (LANES, LANES), 1)
    tri = (r_i <= c_i).astype(bf16)

    def cumsum(m):
        x = m.astype(bf16)
        off = jnp.zeros((ne, 1), f32)
        parts, offs = [], []
        for j in range(nblk):
            c = jnp.dot(x[:, j * LANES:(j + 1) * LANES], tri, preferred_element_type=f32) + off
            offs.append(off)
            parts.append(c)
            off = c[:, LANES - 1:LANES]
        return jnp.concatenate(parts, axis=1), jnp.concatenate(offs, axis=1)

    eq_before = cumsum(eq)[0] - eq.astype(f32)
    sel = gt | (eq & (eq_before < need))
    cum, offs = cumsum(sel)
    pos = jnp.where(sel, cum - 1.0, -1.0)
    pos_ref[0] = pos.astype(jnp.int32)
    st_ref[0] = offs.astype(jnp.int32)
    for j in range(nblk):
        pf_ref[j] = pos[:, j * LANES:(j + 1) * LANES]

    sc = min(256, cap)
    lane = lax.broadcasted_iota(jnp.int32, (sc, LANES), 1)
    cols_ref[...] = jnp.zeros_like(cols_ref)
    for e in range(ne):
        for c0 in range(0, cap, sc):
            slot = (c0 + lax.broadcasted_iota(jnp.int32, (sc, 1), 0)).astype(f32)

            def blk(j, acc):
                tok = (j * LANES + lane[0:1]).astype(f32)
                return acc + jnp.where(pf_ref[j, e:e + 1, :] == slot, tok, 0.0)

            acc = lax.fori_loop(0, nblk, blk, jnp.zeros((sc, LANES), f32))
            col = jnp.sum(acc, axis=1, keepdims=True)
            cols_ref[c0:c0 + sc, :] = jnp.where(lane == e, col, cols_ref[c0:c0 + sc, :])
    idx_ref[0] = cols_ref[...].T[:ne, :cap].astype(jnp.int32)


def _route(aff, cap):
    bsz, ne, n = aff.shape
    assert ne <= LANES
    capp = max(cap, LANES)
    return pl.pallas_call(
        functools.partial(_route_body, cap=cap),
        out_shape=(jax.ShapeDtypeStruct((bsz, ne, cap), jnp.int32), jax.ShapeDtypeStruct((bsz, ne, n), jnp.int32),
                   jax.ShapeDtypeStruct((bsz, ne, n // LANES), jnp.int32)),
        grid=(bsz,),
        in_specs=[pl.BlockSpec((1, ne, n), lambda b: (b, 0, 0))],
        out_specs=(pl.BlockSpec((1, ne, cap), lambda b: (b, 0, 0)), pl.BlockSpec((1, ne, n), lambda b: (b, 0, 0)),
                   pl.BlockSpec((1, ne, n // LANES), lambda b: (b, 0, 0))),
        scratch_shapes=[pltpu.VMEM((n // LANES, ne, LANES), f32), pltpu.VMEM((capp, LANES), f32)],
        compiler_params=_cparams(("arbitrary",)),
        name="route",
    )(aff)


def _row_copy(h_hbm, xs_ref, sem, b, tok, row):
    return pltpu.make_async_copy(h_hbm.at[b, pl.ds(tok, 1)], xs_ref.at[pl.ds(row, 1)], sem)


def _expert_body(idx_ref, h_hbm, wg_ref, wu_ref, wd_ref, o_ref, xs_ref, xb_ref, sem):
    f = pl.program_id(1)
    bsz, _, cap, d = o_ref.shape

    @pl.when(f == 0)
    def _():
        def issue(s, carry):
            for b in range(bsz):
                _row_copy(h_hbm, xs_ref, sem, b, idx_ref[0, b, s], b * cap + s).start()
            return carry

        def drain(s, carry):
            for b in range(bsz):
                _row_copy(h_hbm, xs_ref, sem, b, 0, b * cap + s).wait()
            return carry

        lax.fori_loop(0, cap, issue, 0)
        lax.fori_loop(0, cap, drain, 0)
        xb_ref[...] = xs_ref[...].astype(bf16)

    xb = xb_ref[...]
    a = jnp.dot(xb, wg_ref[0].astype(bf16), preferred_element_type=f32)
    b = jnp.dot(xb, wu_ref[0].astype(bf16), preferred_element_type=f32)
    hmid = (a * (1.0 / (1.0 + jnp.exp(-a))) * b).astype(bf16)
    part = jnp.dot(hmid, wd_ref[0].astype(bf16), preferred_element_type=f32)

    @pl.when(f == 0)
    def _():
        xs_ref[...] = part

    @pl.when(f > 0)
    def _():
        xs_ref[...] += part

    @pl.when(f == pl.num_programs(1) - 1)
    def _():
        o_ref[...] = xs_ref[...].reshape(o_ref.shape).astype(o_ref.dtype)


def _expert_ffn(idx, h, w_gate, w_up, w_down):
    bsz, ne, cap = idx.shape
    d = h.shape[-1]
    ff = w_gate.shape[2]
    tf = 512
    return pl.pallas_call(
        _expert_body,
        out_shape=jax.ShapeDtypeStruct((bsz, ne, cap, d), bf16),
        grid=(ne, ff // tf),
        in_specs=[
            pl.BlockSpec((1, bsz, cap), lambda e, f: (e, 0, 0), memory_space=pltpu.SMEM),
            pl.BlockSpec(memory_space=pl.ANY),
            pl.BlockSpec((1, d, tf), lambda e, f: (e, 0, f)),
            pl.BlockSpec((1, d, tf), lambda e, f: (e, 0, f)),
            pl.BlockSpec((1, tf, d), lambda e, f: (e, f, 0)),
        ],
        out_specs=pl.BlockSpec((bsz, 1, cap, d), lambda e, f: (0, e, 0, 0)),
        scratch_shapes=[pltpu.VMEM((bsz * cap, d), f32), pltpu.VMEM((bsz * cap, d), bf16), pltpu.SemaphoreType.DMA(())],
        compiler_params=_cparams(("arbitrary", "arbitrary")),
        name="expert_ffn",
    )(jnp.swapaxes(idx, 0, 1), h, w_gate, w_up, w_down)


def _window_copy(ye_hbm, win_ref, sem, b, e, start, slot):
    rows = win_ref.shape[1]
    return pltpu.make_async_copy(ye_hbm.at[b, e, pl.ds(start, rows)], win_ref.at[slot], sem.at[slot])


def _combine_body(st_ref, x_ref, g_ref, pos_ref, aff_ref, ye_hbm, o_ref, win_ref, sem):
    b, i = pl.program_id(0), pl.program_id(1)
    ne, tt = pos_ref.shape[1], pos_ref.shape[2]
    rows = win_ref.shape[1]
    last = ye_hbm.shape[2] - rows

    def start(e):
        return pl.multiple_of(jnp.minimum(st_ref[b, e, i] // SUBLANES_BF16 * SUBLANES_BF16, last), SUBLANES_BF16)

    _window_copy(ye_hbm, win_ref, sem, b, 0, start(0), 0).start()
    pos_t = pos_ref[0].astype(f32).T
    aff_t = aff_ref[0].T
    lane = lax.broadcasted_iota(jnp.int32, (1, rows), 1).astype(f32)
    acc = jnp.zeros((tt, x_ref.shape[2]), f32)
    for e in range(ne):
        slot = e % 2
        if e + 1 < ne:
            _window_copy(ye_hbm, win_ref, sem, b, e + 1, start(e + 1), 1 - slot).start()
        _window_copy(ye_hbm, win_ref, sem, b, e, start(e), slot).wait()
        onehot = (pos_t[:, e:e + 1] - start(e).astype(f32) == lane).astype(bf16)
        acc = acc + aff_t[:, e:e + 1] * jnp.dot(onehot, win_ref[slot], preferred_element_type=f32)
    o_ref[0] = x_ref[0] + g_ref[0] * acc


def _combine(x, gate, pos, aff, starts, ye):
    bsz, n, d = x.shape
    ne = pos.shape[1]
    tt = min(256, n)
    rows = min(tt + LANES, ye.shape[2])
    assert ye.shape[2] <= rows or rows >= tt + SUBLANES_BF16
    grid_spec = pltpu.PrefetchScalarGridSpec(
        num_scalar_prefetch=1,
        grid=(bsz, n // tt),
        in_specs=[
            pl.BlockSpec((1, tt, d), lambda b, i, st: (b, i, 0)),
            pl.BlockSpec((1, 1, d), lambda b, i, st: (b, 0, 0)),
            pl.BlockSpec((1, ne, tt), lambda b, i, st: (b, 0, i)),
            pl.BlockSpec((1, ne, tt), lambda b, i, st: (b, 0, i)),
            pl.BlockSpec(memory_space=pl.ANY),
        ],
        out_specs=pl.BlockSpec((1, tt, d), lambda b, i, st: (b, i, 0)),
        scratch_shapes=[pltpu.VMEM((2, rows, d), bf16), pltpu.SemaphoreType.DMA((2,))],
    )
    return pl.pallas_call(
        _combine_body,
        out_shape=jax.ShapeDtypeStruct((bsz, n, d), f32),
        grid_spec=grid_spec,
        compiler_params=_cparams(("arbitrary", "arbitrary")),
        name="moe_combine",
    )(starts, x, gate, pos, aff, ye)


def _moe(x, shift, scale, gate, w_router, w_gate, w_up, w_down):
    bsz, n, d = x.shape
    ne = w_router.shape[1]
    cap = EC_CAPACITY * n // ne
    hn, aff = _moe_pre(x, shift, scale, w_router.T)
    idx, pos, offs = _route(aff, cap)
    ye = _expert_ffn(idx, hn, w_gate, w_up, w_down)
    if cap < LANES:
        ye = jnp.pad(ye, ((0, 0), (0, 0), (0, LANES - cap), (0, 0)))
    tt = min(256, n)
    return _combine(x, gate, pos, aff, offs[:, :, ::tt // LANES], ye)


FFT_A = 128


def _dft_tables(n):
    big = 2 * n
    bn = big // FFT_A
    ha = FFT_A // 2
    a = np.arange(FFT_A)
    b = np.arange(bn)
    ph = (a[None, None, :] * a[None, :, None] * bn + b[:, None, None] * a[None, :, None]) % big
    cr, ci = np.cos(-2 * np.pi * ph / big), np.sin(-2 * np.pi * ph / big)
    m1 = np.concatenate([np.concatenate([cr[:, :, :ha], -ci[:, :, :ha]], 2),
                         np.concatenate([ci[:, :, :ha], cr[:, :, :ha]], 2)], 1)
    m1f = np.concatenate([cr, ci], 1)
    ea = -2 * np.pi * ((b[:, None] * b[None, :]) % bn) / bn
    er, ei = np.cos(ea), np.sin(ea)
    m2 = np.block([[er, -ei], [ei, er]])
    m3 = np.block([[er, ei], [-ei, er]])
    gr = np.transpose(cr[:, :, :ha], (0, 2, 1)) / big
    gi = -np.transpose(ci[:, :, :ha], (0, 2, 1)) / big
    m4 = np.concatenate([np.concatenate([gr, -gi], 2), np.concatenate([gi, gr], 2)], 1)
    return tuple(jnp.asarray(m, bf16) for m in (m1, m1f, m2, m3, m4))


def _dft_tables_small(n):
    big = 2 * n
    k = np.arange(big)
    ang = -2 * np.pi * ((k[:, None] * k[None, :]) % big) / big
    fr, fi = np.cos(ang), np.sin(ang)
    mf = np.concatenate([fr, fi], 0)
    mc = np.block([[fr[:, :n], -fi[:, :n]], [fi[:, :n], fr[:, :n]]])
    gr, gi = fr.T[:n] / big, -fi.T[:n] / big
    mi = np.block([[gr, -gi], [gi, gr]])
    return tuple(jnp.asarray(m, bf16) for m in (mf, mc, mi))


def _hy_inproj_body(xp_ref, x_ref, xn_ref, sh_ref, sc_ref, w_ref, b_ref, cw_ref, cb_ref, o_ref):
    i, nt = pl.program_id(2), pl.num_programs(2)
    tm = x_ref.shape[1]
    xx = jnp.concatenate([xp_ref[0], x_ref[0], xn_ref[0]], axis=0)
    hn = _rms(xx) * (1.0 + sc_ref[0]) + sh_ref[0]
    u0 = jnp.dot(hn.astype(bf16), w_ref[...], preferred_element_type=f32) + b_ref[...]
    row = lax.broadcasted_iota(jnp.int32, u0.shape, 0)
    inside = ((row >= 8) | (i > 0)) & ((row < tm + 8) | (i < nt - 1))
    u0 = jnp.where(inside, u0, 0.0)
    prev = pltpu.roll(u0, 1, axis=0)
    nxt = pltpu.roll(u0, tm + 15, axis=0)
    u = prev * cw_ref[0:1] + u0 * cw_ref[1:2] + nxt * cw_ref[2:3] + cb_ref[...]
    o_ref[0] = u[8:8 + tm]


def _hy_inproj(x, shift, scale, w, bias, conv_w, conv_b):
    bsz, n, d = x.shape
    nn = w.shape[1]
    tm, tn = min(512, n), 1024
    r8 = tm // 8
    return pl.pallas_call(
        _hy_inproj_body,
        out_shape=jax.ShapeDtypeStruct((bsz, n, nn), f32),
        grid=(nn // tn, bsz, n // tm),
        in_specs=[
            pl.BlockSpec((1, 8, d), lambda j, b, i: (b, jnp.maximum(i * r8 - 1, 0), 0)),
            pl.BlockSpec((1, tm, d), lambda j, b, i: (b, i, 0)),
            pl.BlockSpec((1, 8, d), lambda j, b, i: (b, jnp.minimum((i + 1) * r8, n // 8 - 1), 0)),
            pl.BlockSpec((1, 1, d), lambda j, b, i: (b, 0, 0)),
            pl.BlockSpec((1, 1, d), lambda j, b, i: (b, 0, 0)),
            pl.BlockSpec((d, tn), lambda j, b, i: (0, j)),
            pl.BlockSpec((1, tn), lambda j, b, i: (0, j)),
            pl.BlockSpec((HY_SHORT, tn), lambda j, b, i: (0, j)),
            pl.BlockSpec((1, tn), lambda j, b, i: (0, j)),
        ],
        out_specs=pl.BlockSpec((1, tm, tn), lambda j, b, i: (b, i, j)),
        compiler_params=_cparams(("arbitrary", "arbitrary", "arbitrary")),
        name="hy_inproj",
    )(x, x, x, shift, scale, w, bias, conv_w, conv_b)


def _hy_filter_body(f_ref, w1_ref, b1_ref, w2_ref, b2_ref, w3_ref, b3_ref, fr_ref, wo_ref, dl_ref, o_ref, ss_ref, *, n):
    i = pl.program_id(1)
    tp, d = o_ref.shape[1], o_ref.shape[2]

    def taps(shape):
        p = i * tp + lax.broadcasted_iota(jnp.int32, shape, 0)
        tau = jnp.where(p < n, p, 2 * n - p).astype(f32)
        return p, tau

    _, tau = taps((tp, 128))
    lane = lax.broadcasted_iota(jnp.int32, (tp, 128), 1)
    bands = (HY_POS_EMB - 1) // 2
    ang = ((2.0 * math.pi / n) * tau) * f_ref[...]
    feat = jnp.where(lane == 0, tau * (1.0 / (n - 1)),
                     jnp.where(lane <= bands, jnp.cos(ang), jnp.where(lane <= 2 * bands, -jnp.sin(ang), 0.0)))
    fr = fr_ref[...]
    hid = jnp.sin(fr * (jnp.dot(feat, w1_ref[...], precision=HIGHEST, preferred_element_type=f32) + b1_ref[...]))
    hid = jnp.sin(fr * (jnp.dot(hid, w2_ref[...], precision=HIGHEST, preferred_element_type=f32) + b2_ref[...]))
    hid = jnp.sin(fr * (jnp.dot(hid, w3_ref[...], precision=HIGHEST, preferred_element_type=f32) + b3_ref[...]))
    h = jnp.dot(hid, wo_ref[0, 0], precision=HIGHEST, preferred_element_type=f32)
    p, tau = taps((tp, d))
    h = h * jnp.exp(-(tau * (1.0 / (n - 1))) * dl_ref[...])
    h = jnp.where(p == n, 0.0, h)
    o_ref[0] = h

    @pl.when(i == 0)
    def _():
        ss_ref[...] = jnp.zeros_like(ss_ref)

    ss_ref[0] += jnp.sum(h * h, axis=0, keepdims=True)


def _hy_filter(n, w1, b1, w2, b2, w3, b3, freq, w_out):
    d = w_out.shape[1] // (HY_ORDER * 2)
    hid = w1.shape[1]
    tp = 256
    bands = (HY_POS_EMB - 1) // 2
    f = jnp.linspace(1e-4, bands - 1, bands, dtype=f32)
    fvec = jnp.zeros((1, 128), f32).at[0, 1:1 + bands].set(f).at[0, 1 + bands:1 + 2 * bands].set(f)
    w1p = jnp.zeros((128, hid), f32).at[:HY_POS_EMB].set(w1)
    wo = jnp.transpose(w_out.reshape(hid, HY_ORDER, 2, d), (1, 2, 0, 3))
    deltas = jnp.abs(jnp.linspace(HY_MIN_DECAY, HY_MAX_DECAY, d, dtype=f32))[None]
    small = lambda shape: pl.BlockSpec(shape, lambda o, i: (0,) * len(shape))
    return pl.pallas_call(
        functools.partial(_hy_filter_body, n=n),
        out_shape=(jax.ShapeDtypeStruct((HY_ORDER, 2 * n, d), f32), jax.ShapeDtypeStruct((HY_ORDER, 1, d), f32)),
        grid=(HY_ORDER, 2 * n // tp),
        in_specs=[small((1, 128)), small((128, hid)), small((1, hid)), small((hid, hid)), small((1, hid)),
                  small((hid, hid)), small((1, hid)), small((1, hid)),
                  pl.BlockSpec((1, 1, hid, d), lambda o, i: (o, i // (n // tp), 0, 0)), small((1, d))],
        out_specs=(pl.BlockSpec((1, tp, d), lambda o, i: (o, i, 0)), pl.BlockSpec((1, 1, d), lambda o, i: (o, 0, 0))),
        compiler_params=_cparams(("arbitrary", "arbitrary")),
        name="hy_filter",
    )(fvec, w1p, b1[None], w2, b2[None], w3, b3[None], freq[None], wo, deltas)


def _fft1_body(z_ref, m_ref, o_ref):
    half = o_ref.shape[1]

    def body(bp, carry):
        slab = jnp.concatenate([z_ref[0, :, bp, :], z_ref[1, :, bp, :]], axis=0).astype(bf16)
        r = jnp.dot(m_ref[bp], slab, preferred_element_type=f32)
        o_ref[0, :, bp, :] = r[:half]
        o_ref[1, :, bp, :] = r[half:]
        return carry

    lax.fori_loop(0, z_ref.shape[2], body, 0)


def _fft_stage1(z4, coff, m1, d):
    _, ha, bn, _ = z4.shape
    tb = 16
    return pl.pallas_call(
        _fft1_body,
        out_shape=jax.ShapeDtypeStruct((2, FFT_A, bn, d), f32),
        grid=(d // 128, bn // tb),
        in_specs=[pl.BlockSpec((2, ha, tb, 128), lambda c, j: (0, 0, j, c + coff)),
                  pl.BlockSpec((tb, 2 * FFT_A, 2 * ha), lambda c, j: (j, 0, 0))],
        out_specs=pl.BlockSpec((2, FFT_A, tb, 128), lambda c, j: (0, 0, j, c)),
        compiler_params=_cparams(("arbitrary", "arbitrary")),
        name="fft_stage1",
    )(z4, m1)


def _spec_body(t_ref, ss_ref, m2_ref, o_ref):
    scale = lax.rsqrt(ss_ref[0] + EPS)

    def body(k, carry):
        x = jnp.concatenate([t_ref[0, k], t_ref[1, k]], axis=0).astype(bf16)
        o_ref[k] = (jnp.dot(m2_ref[...], x, preferred_element_type=f32) * scale).astype(o_ref.dtype)
        return carry

    lax.fori_loop(0, t_ref.shape[1], body, 0)


def _mid_body(t_ref, h_ref, m2_ref, m3_ref, o_ref):
    bn = t_ref.shape[2]

    def body(k, carry):
        x = jnp.concatenate([t_ref[0, k], t_ref[1, k]], axis=0).astype(bf16)
        zf = jnp.dot(m2_ref[...], x, preferred_element_type=f32)
        h = h_ref[k].astype(f32)
        zr, zi, hr, hi = zf[:bn], zf[bn:], h[:bn], h[bn:]
        y = jnp.concatenate([zr * hr - zi * hi, zr * hi + zi * hr], axis=0).astype(bf16)
        tp = jnp.dot(m3_ref[...], y, preferred_element_type=f32)
        o_ref[0, k] = tp[:bn]
        o_ref[1, k] = tp[bn:]
        return carry

    lax.fori_loop(0, t_ref.shape[1], body, 0)


def _fft_mid(t, m2, m3, spec=None, ss=None):
    _, na, bn, d = t.shape
    kc = 16
    t_spec = pl.BlockSpec((2, kc, bn, 128), lambda c, j: (0, j, 0, c))
    h_spec = pl.BlockSpec((kc, 2 * bn, 128), lambda c, j: (j, 0, c))
    m_spec = pl.BlockSpec((2 * bn, 2 * bn), lambda c, j: (0, 0))
    if ss is not None:
        return pl.pallas_call(
            _spec_body,
            out_shape=jax.ShapeDtypeStruct((na, 2 * bn, d), bf16),
            grid=(d // 128, na // kc),
            in_specs=[t_spec, pl.BlockSpec((1, 1, 128), lambda c, j: (0, 0, c)), m_spec],
            out_specs=h_spec,
            compiler_params=_cparams(("arbitrary", "arbitrary")),
            name="fft_spectrum",
        )(t, ss, m2)
    return pl.pallas_call(
        _mid_body,
        out_shape=jax.ShapeDtypeStruct(t.shape, f32),
        grid=(d // 128, na // kc),
        in_specs=[t_spec, h_spec, m_spec, m_spec],
        out_specs=t_spec,
        compiler_params=_cparams(("arbitrary", "arbitrary")),
        name="fft_mid",
    )(t, spec, m2, m3)


def _ifft_body(t_ref, m_ref, z_ref, g_ref, sk_ref, o_ref):
    ha = z_ref.shape[1]

    def body(bp, carry):
        x = jnp.concatenate([t_ref[0, :, bp, :], t_ref[1, :, bp, :]], axis=0).astype(bf16)
        y = jnp.dot(m_ref[bp], x, preferred_element_type=f32)
        for s in range(2):
            o_ref[s, :, bp, :] = g_ref[s, :, bp, :] * (y[s * ha:(s + 1) * ha] + sk_ref[...] * z_ref[s, :, bp, :])
        return carry

    lax.fori_loop(0, z_ref.shape[2], body, 0)


def _fft_stage4(t, m4, z4, zoff, g4, goff, skip):
    _, na, bn, d = t.shape
    ha = z4.shape[1]
    tb = 16
    return pl.pallas_call(
        _ifft_body,
        out_shape=jax.ShapeDtypeStruct((2, ha, bn, d), f32),
        grid=(d // 128, bn // tb),
        in_specs=[pl.BlockSpec((2, na, tb, 128), lambda c, j: (0, 0, j, c)),
                  pl.BlockSpec((tb, 2 * ha, 2 * na), lambda c, j: (j, 0, 0)),
                  pl.BlockSpec((2, ha, tb, 128), lambda c, j: (0, 0, j, c + zoff)),
                  pl.BlockSpec((2, ha, tb, 128), lambda c, j: (0, 0, j, c + goff)),
                  pl.BlockSpec((1, 128), lambda c, j: (0, c))],
        out_specs=pl.BlockSpec((2, ha, tb, 128), lambda c, j: (0, 0, j, c)),
        compiler_params=_cparams(("arbitrary", "arbitrary")),
        name="fft_stage4",
    )(t, m4, z4, g4, skip)


def _ctx_spec_body(f_ref, ss_ref, mf_ref, o_ref):
    o_ref[0] = (jnp.dot(mf_ref[...], f_ref[0].astype(bf16), preferred_element_type=f32)
                * lax.rsqrt(ss_ref[0] + EPS)).astype(o_ref.dtype)


def _ctx_conv_body(z_ref, g_ref, sk_ref, h_ref, mc_ref, mi_ref, o_ref):
    n = z_ref.shape[1]
    big = 2 * n
    x = jnp.concatenate([z_ref[0], z_ref[1]], axis=0).astype(bf16)
    zf = jnp.dot(mc_ref[...], x, preferred_element_type=f32)
    h = h_ref[0].astype(f32)
    zr, zi, hr, hi = zf[:big], zf[big:], h[:big], h[big:]
    y = jnp.concatenate([zr * hr - zi * hi, zr * hi + zi * hr], axis=0).astype(bf16)
    yt = jnp.dot(mi_ref[...], y, preferred_element_type=f32)
    for s in range(2):
        o_ref[s] = g_ref[s] * (yt[s * n:(s + 1) * n] + sk_ref[...] * z_ref[s])


def _hyena_conv_small(u, full, ss, skip, d):
    bsz, n, _ = u.shape
    mf, mc, mi = _dft_tables_small(n)
    tc = 256
    spec = pl.pallas_call(
        _ctx_spec_body,
        out_shape=jax.ShapeDtypeStruct((HY_ORDER, 4 * n, d), bf16),
        grid=(HY_ORDER, d // tc),
        in_specs=[pl.BlockSpec((1, 2 * n, tc), lambda o, c: (o, 0, c)), pl.BlockSpec((1, 1, tc), lambda o, c: (o, 0, c)),
                  pl.BlockSpec((4 * n, 2 * n), lambda o, c: (0, 0))],
        out_specs=pl.BlockSpec((1, 4 * n, tc), lambda o, c: (o, 0, c)),
        compiler_params=_cparams(("arbitrary", "arbitrary")),
        name="ctx_spectrum",
    )(full, ss, mf)
    z = u
    nct = d // tc
    for o in range(HY_ORDER):
        z = pl.pallas_call(
            _ctx_conv_body,
            out_shape=jax.ShapeDtypeStruct((bsz, n, d), f32),
            grid=(nct,),
            in_specs=[pl.BlockSpec((2, n, tc), lambda c: (0, 0, c)),
                      pl.BlockSpec((2, n, tc), lambda c, o=o: (0, 0, c + (o + 1) * nct)),
                      pl.BlockSpec((1, tc), lambda c: (0, c)),
                      pl.BlockSpec((1, 4 * n, tc), lambda c, o=o: (o, 0, c)),
                      pl.BlockSpec((4 * n, 2 * n), lambda c: (0, 0)),
                      pl.BlockSpec((2 * n, 4 * n), lambda c: (0, 0))],
            out_specs=pl.BlockSpec((2, n, tc), lambda c: (0, 0, c)),
            compiler_params=_cparams(("arbitrary",)),
            name="ctx_conv",
        )(z, u, skip[o:o + 1], spec, mc, mi)
    return z


def _hyena_conv_long(u, full, ss, skip, d):
    bsz, n, c3 = u.shape
    ha = FFT_A // 2
    bn = 2 * n // FFT_A
    m1, m1f, m2, m3, m4 = _dft_tables(n)
    nct = d // 128
    u4 = u.reshape(bsz, ha, bn, c3)
    z4, zoff = u4, 0
    for o in range(HY_ORDER):
        spec = _fft_mid(_fft_stage1(full[o].reshape(2, ha, bn, d), 0, m1f, d), m2, m3, ss=ss[o:o + 1])
        t = _fft_mid(_fft_stage1(z4, zoff, m1, d), m2, m3, spec=spec)
        z4 = _fft_stage4(t, m4, z4, zoff, u4, (o + 1) * nct, skip[o:o + 1])
    return z4.reshape(bsz, n, d)


def _hyena_mixer(x, sh, sc, gate, w_in, b_in, conv_w, conv_b, f_w1, f_b1, f_w2, f_b2, f_w3, f_b3, f_freq, f_wout,
                 skip, w_out, b_out):
    bsz, n, d = x.shape
    assert bsz == 2, "the two samples are packed into one complex sequence"
    full, ss = _hy_filter(n, f_w1, f_b1, f_w2, f_b2, f_w3, f_b3, f_freq, f_wout)
    u = _hy_inproj(x, sh, sc, w_in.astype(bf16), b_in[None], conv_w, conv_b[None])
    conv = _hyena_conv_long if n % (8 * FFT_A) == 0 else _hyena_conv_small
    z = conv(u, full, ss, skip, d)
    return _matmul_residual(z, w_out.astype(bf16), b_out[None], x, gate)


def kernel(x, c, ctx, c_ctx, ada_w, ada_b, sg_w_in, sg_b_in, sg_v_gain, sg_w_s, sg_b_s, sg_w_out, sg_b_out, hy_w_in, hy_b_in, hy_conv_w, hy_conv_b, hy_f_w1, hy_f_b1, hy_f_w2, hy_f_b2, hy_f_w3, hy_f_b3, hy_f_freq, hy_f_wout, hy_skip, hy_w_out, hy_b_out, da_w_in, da_q_gain, da_k_gain, da_lam_q1, da_lam_k1, da_lam_q2, da_lam_k2, da_subln_gain, da_w_out, moe_router, moe_w_gate, moe_w_up, moe_w_down):
    bsz, n, d = x.shape
    depth = ada_w.shape[0]
    cond = jnp.zeros((8, d), f32).at[:bsz].set(c).at[bsz].set(c_ctx)
    mods = _modulation(cond, ada_w, ada_b)

    for i in range(depth):
        kind, j = i % N_MIXERS, i // N_MIXERS
        ctx_live = any(l % N_MIXERS == 2 for l in range(i + 1, depth))
        assert not (kind == 2 and ctx_live), "context-query attention is not needed at this depth and not implemented"
        mx = [mods[i, :bsz, k * d:(k + 1) * d][:, None, :] for k in range(6)]
        mc = [jnp.broadcast_to(mods[i, bsz, k * d:(k + 1) * d], (bsz, 1, d)) for k in range(6)]
        streams = [(x, mx)] + ([(ctx, mc)] if ctx_live else [])
        outs = []
        for h, m in streams:
            sh1, sc1, g1, sh2, sc2, g2 = m
            if kind == 0:
                h = _sg_mixer(h, sh1, sc1, g1, sg_w_in[j], sg_b_in[j], sg_v_gain[j], sg_w_s[j], sg_b_s[j],
                              sg_w_out[j], sg_b_out[j])
            elif kind == 1:
                h = _hyena_mixer(h, sh1, sc1, g1, hy_w_in[j], hy_b_in[j], hy_conv_w[j], hy_conv_b[j], hy_f_w1[j],
                                 hy_f_b1[j], hy_f_w2[j], hy_f_b2[j], hy_f_w3[j], hy_f_b3[j], hy_f_freq[j],
                                 hy_f_wout[j], hy_skip[j], hy_w_out[j], hy_b_out[j])
            else:
                h = _attn_mixer(h, sh1, sc1, g1, ctx, mc[0], mc[1], i, da_w_in[j], da_q_gain[j], da_k_gain[j],
                                da_lam_q1[j], da_lam_k1[j], da_lam_q2[j], da_lam_k2[j], da_subln_gain[j], da_w_out[j])
            h = _moe(h, sh2, sc2, g2, moe_router[i], moe_w_gate[i], moe_w_up[i], moe_w_down[i])
            outs.append(h)
        x = outs[0]
        if ctx_live:
            ctx = outs[1]
    return x
```

```python
import functools
import math

import jax
import jax.numpy as jnp
import numpy as np
from jax import lax
from jax.experimental import pallas as pl
from jax.experimental.pallas import tpu as pltpu

f32 = jnp.float32
bf16 = jnp.bfloat16
HIGHEST = lax.Precision.HIGHEST

EPS = 1e-6
N_MIXERS = 3
CHUNK = 128
SG_GROUPS = 8
HY_ORDER = 2
HY_SHORT = 3
HY_POS_EMB = 33
HY_TARGET = 1e-2
HY_FAST_DECAY_PCT = 0.3
HY_SLOW_DECAY_PCT = 1.5
HY_MAX_DECAY = math.log(HY_TARGET) / HY_FAST_DECAY_PCT
HY_MIN_DECAY = math.log(HY_TARGET) / HY_SLOW_DECAY_PCT
GRID_W = 64
DA_HEADS = 8
ROPE_BASE = 10000.0
EC_CAPACITY = 2

VMEM_LIMIT = 56 * 1024 * 1024


def _cparams(sem):
    return pltpu.CompilerParams(dimension_semantics=sem, vmem_limit_bytes=VMEM_LIMIT)


def _rms(x):
    return x * lax.rsqrt(jnp.mean(x * x, axis=-1, keepdims=True) + EPS)


def _mod_body(c_ref, w_ref, b_ref, o_ref):
    c = c_ref[...]
    s = c * (1.0 / (1.0 + jnp.exp(-c)))
    o_ref[0] = jnp.dot(s, w_ref[0], precision=HIGHEST, preferred_element_type=f32) + b_ref[0]


def _modulation(cond, ada_w, ada_b):
    depth, d, n6 = ada_w.shape
    tn = 1536
    return pl.pallas_call(
        _mod_body,
        out_shape=jax.ShapeDtypeStruct((depth, 8, n6), f32),
        grid=(depth, n6 // tn),
        in_specs=[
            pl.BlockSpec((8, d), lambda i, j: (0, 0)),
            pl.BlockSpec((1, d, tn), lambda i, j: (i, 0, j)),
            pl.BlockSpec((1, 1, tn), lambda i, j: (i, 0, j)),
        ],
        out_specs=pl.BlockSpec((1, 8, tn), lambda i, j: (i, 0, j)),
        compiler_params=_cparams(("arbitrary", "arbitrary")),
        name="modulation",
    )(cond, ada_w, ada_b.reshape(depth, 1, n6))


def _norm_mm_body(x_ref, sh_ref, sc_ref, w_ref, b_ref, o_ref, *, act):
    hn = _rms(x_ref[0]) * (1.0 + sc_ref[0]) + sh_ref[0]
    y = jnp.dot(hn.astype(bf16), w_ref[...], preferred_element_type=f32) + b_ref[...]
    if act == "gelu":
        y = 0.5 * y * (1.0 + lax.erf(y * (2.0 ** -0.5)))
    o_ref[0] = y.astype(o_ref.dtype)


def _norm_matmul(x, shift, scale, w, bias, act, out_dtype, tn):
    bsz, n, d = x.shape
    nn = w.shape[1]
    tm = min(512, n)
    return pl.pallas_call(
        functools.partial(_norm_mm_body, act=act),
        out_shape=jax.ShapeDtypeStruct((bsz, n, nn), out_dtype),
        grid=(nn // tn, bsz, n // tm),
        in_specs=[
            pl.BlockSpec((1, tm, d), lambda j, b, i: (b, i, 0)),
            pl.BlockSpec((1, 1, d), lambda j, b, i: (b, 0, 0)),
            pl.BlockSpec((1, 1, d), lambda j, b, i: (b, 0, 0)),
            pl.BlockSpec((d, tn), lambda j, b, i: (0, j)),
            pl.BlockSpec((1, tn), lambda j, b, i: (0, j)),
        ],
        out_specs=pl.BlockSpec((1, tm, tn), lambda j, b, i: (b, i, j)),
        compiler_params=_cparams(("arbitrary", "arbitrary", "arbitrary")),
        name="norm_matmul",
    )(x, shift, scale, w, bias)


def _mm_res_body(a_ref, w_ref, b_ref, x_ref, g_ref, o_ref):
    y = jnp.dot(a_ref[0].astype(bf16), w_ref[...], preferred_element_type=f32) + b_ref[...]
    o_ref[0] = x_ref[0] + g_ref[0] * y


def _matmul_residual(a, w, bias, x, gate):
    bsz, n, k = a.shape
    d = w.shape[1]
    tm = min(512, n)
    return pl.pallas_call(
        _mm_res_body,
        out_shape=jax.ShapeDtypeStruct((bsz, n, d), f32),
        grid=(bsz, n // tm),
        in_specs=[
            pl.BlockSpec((1, tm, k), lambda b, i: (b, i, 0)),
            pl.BlockSpec((k, d), lambda b, i: (0, 0)),
            pl.BlockSpec((1, d), lambda b, i: (0, 0)),
            pl.BlockSpec((1, tm, d), lambda b, i: (b, i, 0)),
            pl.BlockSpec((1, 1, d), lambda b, i: (b, 0, 0)),
        ],
        out_specs=pl.BlockSpec((1, tm, d), lambda b, i: (b, i, 0)),
        compiler_params=_cparams(("arbitrary", "arbitrary")),
        name="matmul_residual",
    )(a, w, bias, x, gate)


def _sg_body(u_ref, v_ref, vg_ref, ws_ref, bs_ref, wo_ref, bo_ref, x_ref, g_ref, o_ref, gated_ref):
    tm = u_ref.shape[1]
    gw = u_ref.shape[2] // SG_GROUPS
    v = (_rms(v_ref[0].astype(f32)) * vg_ref[...]).astype(bf16)
    for ch in range(tm // CHUNK):
        rows = slice(ch * CHUNK, (ch + 1) * CHUNK)
        for g in range(SG_GROUPS):
            cols = slice(g * gw, (g + 1) * gw)
            s = jnp.dot(ws_ref[g], v[rows, cols], preferred_element_type=f32) + bs_ref[g]
            gated_ref[rows, cols] = (u_ref[0, rows, cols].astype(f32) * s).astype(bf16)
    y = jnp.dot(gated_ref[...], wo_ref[...], preferred_element_type=f32) + bo_ref[...]
    o_ref[0] = x_ref[0] + g_ref[0] * y


def _sg_gate_out(zz, v_gain, w_s, b_s, w_out, b_out, x, gate):
    bsz, n, w2 = zz.shape
    wd = w2 // 2
    d = x.shape[-1]
    tm = min(256, n)
    return pl.pallas_call(
        _sg_body,
        out_shape=jax.ShapeDtypeStruct((bsz, n, d), f32),
        grid=(bsz, n // tm),
        in_specs=[
            pl.BlockSpec((1, tm, wd), lambda b, i: (b, i, 0)),
            pl.BlockSpec((1, tm, wd), lambda b, i: (b, i, 1)),
            pl.BlockSpec((1, wd), lambda b, i: (0, 0)),
            pl.BlockSpec((SG_GROUPS, CHUNK, CHUNK), lambda b, i: (0, 0, 0)),
            pl.BlockSpec((SG_GROUPS, CHUNK, 1), lambda b, i: (0, 0, 0)),
            pl.BlockSpec((wd, d), lambda b, i: (0, 0)),
            pl.BlockSpec((1, d), lambda b, i: (0, 0)),
            pl.BlockSpec((1, tm, d), lambda b, i: (b, i, 0)),
            pl.BlockSpec((1, 1, d), lambda b, i: (b, 0, 0)),
        ],
        out_specs=pl.BlockSpec((1, tm, d), lambda b, i: (b, i, 0)),
        scratch_shapes=[pltpu.VMEM((tm, wd), bf16)],
        compiler_params=_cparams(("arbitrary", "arbitrary")),
        name="sg_gate_out",
    )(zz, zz, v_gain, w_s, b_s, w_out, b_out, x, gate)


def _sg_mixer(x, sh, sc, gate, w_in, b_in, v_gain, w_s, b_s, w_out, b_out):
    zz = _norm_matmul(x, sh, sc, w_in.astype(bf16), b_in[None], "gelu", bf16, 2048)
    return _sg_gate_out(zz, v_gain[None], w_s.astype(bf16), b_s[:, :, None], w_out.astype(bf16), b_out[None], x, gate)


def _qk_prep_body(q_ref, k_ref, v_ref, cs_ref, qg_ref, kg_ref, bd_ref, qo_ref, ko_ref, vo_ref, *, rope, dh):
    tm, dq = q_ref.shape[1], q_ref.shape[2]
    reps = dq // 128

    def group_ms(x):
        x2 = x * x
        hi = x2.astype(bf16)
        lo = (x2 - hi.astype(f32)).astype(bf16)
        cols = []
        for j in range(reps):
            sl = slice(j * 128, (j + 1) * 128)
            cols.append(jnp.dot(hi[:, sl], bd_ref[...], preferred_element_type=f32)
                        + jnp.dot(lo[:, sl], bd_ref[...], preferred_element_type=f32))
        return jnp.concatenate(cols, axis=1) * (1.0 / dh)

    def prep(x, gain):
        y = x * lax.rsqrt(group_ms(x) + EPS) * gain
        if rope:
            cos = jnp.concatenate([cs_ref[0]] * reps, axis=1)
            sin = jnp.concatenate([cs_ref[1]] * reps, axis=1)
            lane = lax.broadcasted_iota(jnp.int32, y.shape, 1)
            quarter = dh // 4
            partner = jnp.where((lane & quarter) != 0, pltpu.roll(y, quarter, axis=1), pltpu.roll(y, dq - quarter, axis=1))
            y = y * cos + partner * sin
        return y

    qo_ref[0] = (prep(q_ref[0], qg_ref[...]) * (dh ** -0.5)).astype(bf16)
    ko_ref[0] = prep(k_ref[0], kg_ref[...]).astype(bf16)
    vo_ref[0] = v_ref[0].astype(bf16)


def _qk_prep(qkv, cs, q_gain, k_gain, rope):
    bsz, n, d3 = qkv.shape
    dq = d3 // 3
    dh = q_gain.shape[0]
    tm = min(512, n)
    lane = jnp.arange(128)
    bd = (lane[:, None] // dh == lane[None, :] // dh).astype(bf16)
    qg = jnp.tile(q_gain, dq // dh)[None]
    kg = jnp.tile(k_gain, dq // dh)[None]
    out = jax.ShapeDtypeStruct((bsz, n, dq), bf16)
    return pl.pallas_call(
        functools.partial(_qk_prep_body, rope=rope, dh=dh),
        out_shape=(out, out, out),
        grid=(bsz, n // tm),
        in_specs=[
            pl.BlockSpec((1, tm, dq), lambda b, i: (b, i, 0)),
            pl.BlockSpec((1, tm, dq), lambda b, i: (b, i, 1)),
            pl.BlockSpec((1, tm, dq), lambda b, i: (b, i, 2)),
            pl.BlockSpec((2, tm, 128), lambda b, i: (0, i, 0)),
            pl.BlockSpec((1, dq), lambda b, i: (0, 0)),
            pl.BlockSpec((1, dq), lambda b, i: (0, 0)),
            pl.BlockSpec((128, 128), lambda b, i: (0, 0)),
        ],
        out_specs=tuple(pl.BlockSpec((1, tm, dq), lambda b, i: (b, i, 0)) for _ in range(3)),
        compiler_params=_cparams(("arbitrary", "arbitrary")),
        name="qk_prep",
    )(qkv, qkv, qkv, cs, qg, kg, bd)


def _attn_body(lam_ref, q_ref, k_ref, v_ref, sg_ref, o_ref, qq_ref, s_ref, acc_ref, *, tk, lam_init):
    tq, dv = q_ref.shape[1], q_ref.shape[2]
    nk = k_ref.shape[1]
    dh = dv // 2
    lq1, lk1, lq2, lk2 = lam_ref[0:1], lam_ref[1:2], lam_ref[2:3], lam_ref[3:4]
    lam = jnp.exp(jnp.sum(lq1 * lk1, keepdims=True)) - jnp.exp(jnp.sum(lq2 * lk2, keepdims=True)) + lam_init
    q = q_ref[0]
    lane = lax.broadcasted_iota(jnp.int32, q.shape, 1)
    zero = jnp.zeros_like(q)
    qq_ref[:tq] = jnp.where(lane < dh, q, zero)
    qq_ref[tq:] = jnp.where(lane >= dh, q, zero)
    acc_ref[...] = jnp.zeros_like(acc_ref)
    nchunks = nk // tk

    def scores(j, buf):
        kb = k_ref[0, pl.ds(j * tk, tk), :]
        s_ref[buf] = lax.dot_general(qq_ref[...], kb, (((1,), (1,)), ((), ())), preferred_element_type=f32)

    def update(j, buf, ml):
        vb = v_ref[0, pl.ds(j * tk, tk), :]
        out = []
        for h in range(2):
            rows = pl.ds(h * tq, tq)
            m_old, l_old = ml[h]
            s = s_ref[buf, rows, :]
            m_new = jnp.maximum(m_old, jnp.max(s, axis=-1, keepdims=True))
            alpha = jnp.exp(m_old - m_new)
            p = jnp.exp(s - m_new)
            out.append((m_new, alpha * l_old + jnp.sum(p, axis=-1, keepdims=True)))
            acc_ref[rows, :] = alpha * acc_ref[rows, :] + jnp.dot(p.astype(bf16), vb, preferred_element_type=f32)
        return tuple(out)

    scores(0, 0)

    def step(jj, ml):
        j = 2 * jj
        scores(j + 1, 1)
        ml = update(j, 0, ml)
        scores(j + 2, 0)
        return update(j + 1, 1, ml)

    one = (jnp.full((tq, 1), -jnp.inf, f32), jnp.zeros((tq, 1), f32))
    ml = lax.fori_loop(0, (nchunks - 1) // 2, step, (one, one))
    if nchunks % 2 == 0:
        scores(nchunks - 1, 1)
        ml = update(nchunks - 2, 0, ml)
        ml = update(nchunks - 1, 1, ml)
    else:
        ml = update(nchunks - 1, 0, ml)
    l = jnp.concatenate([ml[0][1], ml[1][1]], axis=0)
    o = acc_ref[...] / l
    o = o[:tq] - lam * o[tq:]
    o_ref[0] = (_rms(o) * sg_ref[...]).astype(o_ref.dtype)


def _diff_attention(q, k, v, lam_vecs, subln, lam_init):
    bsz, n, dq = q.shape
    nk = k.shape[1]
    dv = dq // DA_HEADS
    tq = 256
    tk = 768
    assert nk % tk == 0
    return pl.pallas_call(
        functools.partial(_attn_body, tk=tk, lam_init=lam_init),
        out_shape=jax.ShapeDtypeStruct((bsz, n, dq), bf16),
        grid=(bsz, DA_HEADS, n // tq),
        in_specs=[
            pl.BlockSpec((4, dv // 2), lambda b, h, i: (0, 0)),
            pl.BlockSpec((1, tq, dv), lambda b, h, i: (b, i, h)),
            pl.BlockSpec((1, nk, dv), lambda b, h, i: (b, 0, h)),
            pl.BlockSpec((1, nk, dv), lambda b, h, i: (b, 0, h)),
            pl.BlockSpec((1, dv), lambda b, h, i: (0, 0)),
        ],
        out_specs=pl.BlockSpec((1, tq, dv), lambda b, h, i: (b, i, h)),
        scratch_shapes=[pltpu.VMEM((2 * tq, dv), bf16), pltpu.VMEM((2, 2 * tq, tk), f32), pltpu.VMEM((2 * tq, dv), f32)],
        compiler_params=_cparams(("arbitrary", "arbitrary", "arbitrary")),
        name="diff_attention",
    )(lam_vecs, q, k, v, subln)


def _rope_tables(n, dh):
    quarter = dh // 4
    t = jnp.arange(n)
    row = (t // GRID_W).astype(f32)
    col = (t % GRID_W).astype(f32)
    inv = ROPE_BASE ** (-jnp.arange(quarter, dtype=f32) / quarter)
    j = jnp.arange(128) % dh
    ax, half, fr = j // (2 * quarter), (j % (2 * quarter)) // quarter, j % quarter
    ang = jnp.where(ax[None, :] == 0, row[:, None], col[:, None]) * inv[fr][None, :]
    sign = jnp.where(half == 0, -1.0, 1.0)[None, :]
    return jnp.stack([jnp.cos(ang), jnp.sin(ang) * sign])


def _attn_mixer(x, sh, sc, gate, ctx, csh, csc, layer_idx, w_in, q_gain, k_gain, lq1, lk1, lq2, lk2, subln, w_out):
    bsz, n, d = x.shape
    dh = q_gain.shape[0]
    w = w_in.astype(bf16)
    zero_b = jnp.zeros((1, w.shape[1]), f32)
    lam_init = 0.8 - 0.6 * math.exp(-0.3 * layer_idx)
    qkv_x = _norm_matmul(x, sh, sc, w, zero_b, None, f32, 1024)
    qkv_c = _norm_matmul(ctx, csh, csc, w, zero_b, None, f32, 1024)
    cs = _rope_tables(n, dh)
    qx, kx, vx = _qk_prep(qkv_x, cs, q_gain, k_gain, True)
    _, kc, vc = _qk_prep(qkv_c, cs, q_gain, k_gain, False)
    k_all = jnp.concatenate([kc, kx], axis=1)
    v_all = jnp.concatenate([vc, vx], axis=1)
    o = _diff_attention(qx, k_all, v_all, jnp.stack([lq1, lk1, lq2, lk2]), (subln * (1.0 - lam_init))[None], lam_init)
    return _matmul_residual(o, w_out.astype(bf16), jnp.zeros((1, d), f32), x, gate)


def _moe_pre_body(x_ref, sh_ref, sc_ref, wr_ref, h_ref, a_ref):
    hn = _rms(x_ref[0]) * (1.0 + sc_ref[0]) + sh_ref[0]
    half = hn.shape[1] // 2
    bits = pltpu.bitcast(hn.astype(bf16).astype(f32), jnp.int32)
    h_ref[0] = (bits[:, :half] & jnp.int32(-65536)) | lax.shift_right_logical(bits[:, half:], 16)
    lg = lax.dot_general(wr_ref[...], hn, (((1,), (1,)), ((), ())), precision=HIGHEST, preferred_element_type=f32)
    e = jnp.exp(lg - jnp.max(lg, axis=0, keepdims=True))
    a_ref[0] = e / jnp.sum(e, axis=0, keepdims=True)


def _moe_pre(x, shift, scale, w_router_t):
    bsz, n, d = x.shape
    ne = w_router_t.shape[0]
    tm = min(512, n)
    return pl.pallas_call(
        _moe_pre_body,
        out_shape=(jax.ShapeDtypeStruct((bsz, n, d // 2), jnp.int32), jax.ShapeDtypeStruct((bsz, ne, n), f32)),
        grid=(bsz, n // tm),
        in_specs=[
            pl.BlockSpec((1, tm, d), lambda b, i: (b, i, 0)),
            pl.BlockSpec((1, 1, d), lambda b, i: (b, 0, 0)),
            pl.BlockSpec((1, 1, d), lambda b, i: (b, 0, 0)),
            pl.BlockSpec((ne, d), lambda b, i: (0, 0)),
        ],
        out_specs=(pl.BlockSpec((1, tm, d // 2), lambda b, i: (b, i, 0)), pl.BlockSpec((1, ne, tm), lambda b, i: (b, 0, i))),
        compiler_params=_cparams(("arbitrary", "arbitrary")),
        name="moe_pre",
    )(x, shift, scale, w_router_t)


LANES = 128
SUBLANES_BF16 = 16


def _route_body(a_ref, idx_ref, pos_ref, st_ref, pf_ref, cols_ref, *, cap):
    ne, n = a_ref.shape[1], a_ref.shape[2]
    nblk = n // LANES
    bits = pltpu.bitcast(a_ref[0], jnp.int32)

    def bisect(i, thr):
        cand = thr | lax.shift_left(jnp.int32(1), 30 - i)
        cnt = jnp.sum((bits >= cand).astype(f32), axis=1, keepdims=True)
        return jnp.where(cnt >= cap, cand, thr)

    thr = lax.fori_loop(0, 31, bisect, jnp.zeros((ne, 1), jnp.int32))
    gt, eq = bits > thr, bits == thr
    need = cap - jnp.sum(gt.astype(f32), axis=1, keepdims=True)
    r_i = lax.broadcasted_iota(jnp.int32, (LANES, LANES), 0)
    c_i = lax.broadcasted_iota(jnp.int32, (LANES, LANES), 1)
    tri = (r_i <= c_i).astype(bf16)

    def cumsum(m):
        x = m.astype(bf16)
        off = jnp.zeros((ne, 1), f32)
        parts, offs = [], []
        for j in range(nblk):
            c = jnp.dot(x[:, j * LANES:(j + 1) * LANES], tri, preferred_element_type=f32) + off
            offs.append(off)
            parts.append(c)
            off = c[:, LANES - 1:LANES]
        return jnp.concatenate(parts, axis=1), jnp.concatenate(offs, axis=1)

    eq_before = cumsum(eq)[0] - eq.astype(f32)
    sel = gt | (eq & (eq_before < need))
    cum, offs = cumsum(sel)
    pos = jnp.where(sel, cum - 1.0, -1.0)
    pos_ref[0] = pos.astype(jnp.int32)
    st_ref[0] = offs.astype(jnp.int32)
    for j in range(nblk):
        pf_ref[j] = pos[:, j * LANES:(j + 1) * LANES]

    sc = min(256, cap)
    lane = lax.broadcasted_iota(jnp.int32, (sc, LANES), 1)
    cols_ref[...] = jnp.zeros_like(cols_ref)
    for e in range(ne):
        for c0 in range(0, cap, sc):
            slot = (c0 + lax.broadcasted_iota(jnp.int32, (sc, 1), 0)).astype(f32)

            def blk(j, acc):
                tok = (j * LANES + lane[0:1]).astype(f32)
                return acc + jnp.where(pf_ref[j, e:e + 1, :] == slot, tok, 0.0)

            acc = lax.fori_loop(0, nblk, blk, jnp.zeros((sc, LANES), f32))
            col = jnp.sum(acc, axis=1, keepdims=True)
            cols_ref[c0:c0 + sc, :] = jnp.where(lane == e, col, cols_ref[c0:c0 + sc, :])
    idx_ref[0] = cols_ref[...].T[:ne, :cap].astype(jnp.int32)


def _route(aff, cap):
    bsz, ne, n = aff.shape
    assert ne <= LANES
    capp = max(cap, LANES)
    return pl.pallas_call(
        functools.partial(_route_body, cap=cap),
        out_shape=(jax.ShapeDtypeStruct((bsz, ne, cap), jnp.int32), jax.ShapeDtypeStruct((bsz, ne, n), jnp.int32),
                   jax.ShapeDtypeStruct((bsz, ne, n // LANES), jnp.int32)),
        grid=(bsz,),
        in_specs=[pl.BlockSpec((1, ne, n), lambda b: (b, 0, 0))],
        out_specs=(pl.BlockSpec((1, ne, cap), lambda b: (b, 0, 0)), pl.BlockSpec((1, ne, n), lambda b: (b, 0, 0)),
                   pl.BlockSpec((1, ne, n // LANES), lambda b: (b, 0, 0))),
        scratch_shapes=[pltpu.VMEM((n // LANES, ne, LANES), f32), pltpu.VMEM((capp, LANES), f32)],
        compiler_params=_cparams(("arbitrary",)),
        name="route",
    )(aff)


def _row_copy(h_hbm, g_ref, sem, b, tok, buf, row):
    return pltpu.make_async_copy(h_hbm.at[b, pl.ds(tok, 1)], g_ref.at[buf, pl.ds(row, 1)], sem.at[buf])


def _expert_body(idx_ref, idxn_ref, h_hbm, wg_ref, wu_ref, wd_ref, o_ref, g_ref, xb_ref, acc_ref, sem):
    e, f = pl.program_id(0), pl.program_id(1)
    ne, nf = pl.num_programs(0), pl.num_programs(1)
    bsz, _, cap, d = o_ref.shape
    rows, half = g_ref.shape[1], g_ref.shape[2]
    per = rows // nf
    cur = e % 2

    def drain(buf):
        def body(r, carry):
            _row_copy(h_hbm, g_ref, sem, 0, 0, buf, r).wait()
            return carry
        lax.fori_loop(0, rows, body, 0)

    @pl.when((e == 0) & (f == 0))
    def _():
        def issue(s, carry):
            for b in range(bsz):
                _row_copy(h_hbm, g_ref, sem, b, idx_ref[0, b, s], 0, b * cap + s).start()
            return carry
        lax.fori_loop(0, cap, issue, 0)

    @pl.when(f == 0)
    def _():
        drain(cur)
        w = g_ref[cur]
        xb_ref[:, :half] = pltpu.bitcast(w & jnp.int32(-65536), f32).astype(bf16)
        xb_ref[:, half:] = pltpu.bitcast(lax.shift_left(w, 16), f32).astype(bf16)

    b0 = (f * per) // cap
    s0 = (f * per) % cap
    for k in range(per):
        _row_copy(h_hbm, g_ref, sem, b0, idxn_ref[0, b0, s0 + k], 1 - cur, f * per + k).start()

    xb = xb_ref[...]
    a = jnp.dot(xb, wg_ref[0, 0].astype(bf16), preferred_element_type=f32)
    b = jnp.dot(xb, wu_ref[0, 0].astype(bf16), preferred_element_type=f32)
    hmid = (a * (1.0 / (1.0 + jnp.exp(-a))) * b).astype(bf16)
    part = jnp.dot(hmid, wd_ref[0, 0].astype(bf16), preferred_element_type=f32)

    @pl.when(f == 0)
    def _():
        acc_ref[...] = part

    @pl.when(f > 0)
    def _():
        acc_ref[...] += part

    @pl.when(f == nf - 1)
    def _():
        o_ref[...] = acc_ref[...].reshape(o_ref.shape).astype(o_ref.dtype)

    @pl.when((e == ne - 1) & (f == nf - 1))
    def _():
        drain(1 - cur)


def _expert_ffn(idx, h, layer, w_gate, w_up, w_down):
    bsz, ne, cap = idx.shape
    half = h.shape[-1]
    d = 2 * half
    ff = w_gate.shape[3]
    tf = 512
    nf = ff // tf
    rows = bsz * cap
    assert rows % nf == 0 and cap % (rows // nf) == 0
    idx_t = jnp.swapaxes(idx, 0, 1)
    return pl.pallas_call(
        _expert_body,
        out_shape=jax.ShapeDtypeStruct((bsz, ne, cap, d), bf16),
        grid=(ne, nf),
        in_specs=[
            pl.BlockSpec((1, bsz, cap), lambda e, f: (e, 0, 0), memory_space=pltpu.SMEM),
            pl.BlockSpec((1, bsz, cap), lambda e, f: (jnp.minimum(e + 1, ne - 1), 0, 0), memory_space=pltpu.SMEM),
            pl.BlockSpec(memory_space=pl.ANY),
            pl.BlockSpec((1, 1, d, tf), lambda e, f: (layer, e, 0, f)),
            pl.BlockSpec((1, 1, d, tf), lambda e, f: (layer, e, 0, f)),
            pl.BlockSpec((1, 1, tf, d), lambda e, f: (layer, e, f, 0)),
        ],
        out_specs=pl.BlockSpec((bsz, 1, cap, d), lambda e, f: (0, e, 0, 0)),
        scratch_shapes=[pltpu.VMEM((2, rows, half), jnp.int32), pltpu.VMEM((rows, d), bf16), pltpu.VMEM((rows, d), f32),
                        pltpu.SemaphoreType.DMA((2,))],
        compiler_params=_cparams(("arbitrary", "arbitrary")),
        name="expert_ffn",
    )(idx_t, idx_t, h, w_gate, w_up, w_down)


def _window_copy(ye_hbm, win_ref, sem, b, e, start, slot):
    rows = win_ref.shape[1]
    return pltpu.make_async_copy(ye_hbm.at[b, e, pl.ds(start, rows)], win_ref.at[slot], sem.at[slot])


def _combine_body(st_ref, x_ref, g_ref, pos_ref, aff_ref, ye_hbm, o_ref, win_ref, acc_ref, sem):
    b, i = pl.program_id(0), pl.program_id(1)
    ne, tt = pos_ref.shape[1], pos_ref.shape[2]
    rows = win_ref.shape[1]
    last = ye_hbm.shape[2] - rows

    def aligned(s):
        return pl.multiple_of(jnp.minimum(s // SUBLANES_BF16 * SUBLANES_BF16, last), SUBLANES_BF16)

    def start(e):
        return aligned(st_ref[b, e, i])

    _window_copy(ye_hbm, win_ref, sem, b, 0, start(0), 0).start()
    pos_t = pos_ref[0].astype(f32).T
    aff_t = aff_ref[0].T
    lane = lax.broadcasted_iota(jnp.int32, (1, rows), 1).astype(f32)
    for e in range(ne):
        slot = e % 2
        if e + 1 < ne:
            _window_copy(ye_hbm, win_ref, sem, b, e + 1, start(e + 1), 1 - slot).start()
        _window_copy(ye_hbm, win_ref, sem, b, e, start(e), slot).wait()
        pos_e = pos_t[:, e:e + 1]
        aff_e = aff_t[:, e:e + 1]
        ws = start(e)
        part = aff_e * jnp.dot((pos_e - ws.astype(f32) == lane).astype(bf16), win_ref[slot], preferred_element_type=f32)
        if e == 0:
            acc_ref[...] = part
        else:
            acc_ref[...] += part

        def extra(c, carry):
            lo = ws + (c + 1) * rows
            cs = aligned(lo)
            cp = _window_copy(ye_hbm, win_ref, sem, b, e, cs, 2)
            cp.start()
            cp.wait()
            hit = (pos_e - cs.astype(f32) == lane) & (pos_e >= lo.astype(f32))
            acc_ref[...] += aff_e * jnp.dot(hit.astype(bf16), win_ref[2], preferred_element_type=f32)
            return carry

        lax.fori_loop(0, jnp.maximum(st_ref[b, e, i + 1] - ws - 1, 0) // rows, extra, 0)
    o_ref[0] = x_ref[0] + g_ref[0] * acc_ref[...]


def _combine(x, gate, pos, aff, starts, ye):
    bsz, n, d = x.shape
    ne = pos.shape[1]
    tt = min(256, n)
    rows = LANES
    assert ye.shape[2] >= rows and ye.shape[2] % SUBLANES_BF16 == 0
    grid_spec = pltpu.PrefetchScalarGridSpec(
        num_scalar_prefetch=1,
        grid=(bsz, n // tt),
        in_specs=[
            pl.BlockSpec((1, tt, d), lambda b, i, st: (b, i, 0)),
            pl.BlockSpec((1, 1, d), lambda b, i, st: (b, 0, 0)),
            pl.BlockSpec((1, ne, tt), lambda b, i, st: (b, 0, i)),
            pl.BlockSpec((1, ne, tt), lambda b, i, st: (b, 0, i)),
            pl.BlockSpec(memory_space=pl.ANY),
        ],
        out_specs=pl.BlockSpec((1, tt, d), lambda b, i, st: (b, i, 0)),
        scratch_shapes=[pltpu.VMEM((3, rows, d), bf16), pltpu.VMEM((tt, d), f32), pltpu.SemaphoreType.DMA((3,))],
    )
    return pl.pallas_call(
        _combine_body,
        out_shape=jax.ShapeDtypeStruct((bsz, n, d), f32),
        grid_spec=grid_spec,
        compiler_params=_cparams(("arbitrary", "arbitrary")),
        name="moe_combine",
    )(starts, x, gate, pos, aff, ye)


def _moe(x, shift, scale, gate, w_router, layer, w_gate, w_up, w_down):
    bsz, n, d = x.shape
    ne = w_router.shape[1]
    cap = EC_CAPACITY * n // ne
    hn, aff = _moe_pre(x, shift, scale, w_router.T)
    idx, pos, offs = _route(aff, cap)
    ye = _expert_ffn(idx, hn, layer, w_gate, w_up, w_down)
    if cap < LANES:
        ye = jnp.pad(ye, ((0, 0), (0, 0), (0, LANES - cap), (0, 0)))
    tt = min(256, n)
    starts = jnp.concatenate([offs[:, :, ::tt // LANES], jnp.full((bsz, ne, 1), cap, jnp.int32)], axis=-1)
    return _combine(x, gate, pos, aff, starts, ye)


FFT_A = 128


def _dft_tables(n):
    big = 2 * n
    bn = big // FFT_A
    ha = FFT_A // 2
    a = np.arange(FFT_A)
    b = np.arange(bn)
    ph = (a[None, None, :] * a[None, :, None] * bn + b[:, None, None] * a[None, :, None]) % big
    cr, ci = np.cos(-2 * np.pi * ph / big), np.sin(-2 * np.pi * ph / big)
    m1 = np.concatenate([np.concatenate([cr[:, :, :ha], -ci[:, :, :ha]], 2),
                         np.concatenate([ci[:, :, :ha], cr[:, :, :ha]], 2)], 1)
    m1f = np.concatenate([cr, ci], 1)
    ea = -2 * np.pi * ((b[:, None] * b[None, :]) % bn) / bn
    er, ei = np.cos(ea), np.sin(ea)
    m2 = np.block([[er, -ei], [ei, er]])
    m3 = np.block([[er, ei], [-ei, er]])
    gr = np.transpose(cr[:, :, :ha], (0, 2, 1)) / big
    gi = -np.transpose(ci[:, :, :ha], (0, 2, 1)) / big
    m4 = np.concatenate([np.concatenate([gr, -gi], 2), np.concatenate([gi, gr], 2)], 1)
    return tuple(jnp.asarray(m, bf16) for m in (m1, m1f, m2, m3, m4))


def _dft_tables_small(n):
    big = 2 * n
    k = np.arange(big)
    ang = -2 * np.pi * ((k[:, None] * k[None, :]) % big) / big
    fr, fi = np.cos(ang), np.sin(ang)
    mf = np.concatenate([fr, fi], 0)
    mc = np.block([[fr[:, :n], -fi[:, :n]], [fi[:, :n], fr[:, :n]]])
    gr, gi = fr.T[:n] / big, -fi.T[:n] / big
    mi = np.block([[gr, -gi], [gi, gr]])
    return tuple(jnp.asarray(m, bf16) for m in (mf, mc, mi))


def _hy_inproj_body(xp_ref, x_ref, xn_ref, sh_ref, sc_ref, w_ref, b_ref, cw_ref, cb_ref, o_ref):
    i, nt = pl.program_id(2), pl.num_programs(2)
    tm = x_ref.shape[1]
    xx = jnp.concatenate([xp_ref[0], x_ref[0], xn_ref[0]], axis=0)
    hn = _rms(xx) * (1.0 + sc_ref[0]) + sh_ref[0]
    u0 = jnp.dot(hn.astype(bf16), w_ref[...], preferred_element_type=f32) + b_ref[...]
    row = lax.broadcasted_iota(jnp.int32, u0.shape, 0)
    inside = ((row >= 8) | (i > 0)) & ((row < tm + 8) | (i < nt - 1))
    u0 = jnp.where(inside, u0, 0.0)
    prev = pltpu.roll(u0, 1, axis=0)
    nxt = pltpu.roll(u0, tm + 15, axis=0)
    u = prev * cw_ref[0:1] + u0 * cw_ref[1:2] + nxt * cw_ref[2:3] + cb_ref[...]
    o_ref[0] = u[8:8 + tm]


def _hy_inproj(x, shift, scale, w, bias, conv_w, conv_b):
    bsz, n, d = x.shape
    nn = w.shape[1]
    tm, tn = min(512, n), 1024
    r8 = tm // 8
    return pl.pallas_call(
        _hy_inproj_body,
        out_shape=jax.ShapeDtypeStruct((bsz, n, nn), f32),
        grid=(nn // tn, bsz, n // tm),
        in_specs=[
            pl.BlockSpec((1, 8, d), lambda j, b, i: (b, jnp.maximum(i * r8 - 1, 0), 0)),
            pl.BlockSpec((1, tm, d), lambda j, b, i: (b, i, 0)),
            pl.BlockSpec((1, 8, d), lambda j, b, i: (b, jnp.minimum((i + 1) * r8, n // 8 - 1), 0)),
            pl.BlockSpec((1, 1, d), lambda j, b, i: (b, 0, 0)),
            pl.BlockSpec((1, 1, d), lambda j, b, i: (b, 0, 0)),
            pl.BlockSpec((d, tn), lambda j, b, i: (0, j)),
            pl.BlockSpec((1, tn), lambda j, b, i: (0, j)),
            pl.BlockSpec((HY_SHORT, tn), lambda j, b, i: (0, j)),
            pl.BlockSpec((1, tn), lambda j, b, i: (0, j)),
        ],
        out_specs=pl.BlockSpec((1, tm, tn), lambda j, b, i: (b, i, j)),
        compiler_params=_cparams(("arbitrary", "arbitrary", "arbitrary")),
        name="hy_inproj",
    )(x, x, x, shift, scale, w, bias, conv_w, conv_b)


def _hy_filter_body(f_ref, w1_ref, b1_ref, w2_ref, b2_ref, w3_ref, b3_ref, fr_ref, wo_ref, dl_ref, o_ref, ss_ref, *, n):
    i = pl.program_id(1)
    tp, d = o_ref.shape[1], o_ref.shape[2]

    def taps(shape):
        p = i * tp + lax.broadcasted_iota(jnp.int32, shape, 0)
        tau = jnp.where(p < n, p, 2 * n - p).astype(f32)
        return p, tau

    _, tau = taps((tp, 128))
    lane = lax.broadcasted_iota(jnp.int32, (tp, 128), 1)
    bands = (HY_POS_EMB - 1) // 2
    ang = ((2.0 * math.pi / n) * tau) * f_ref[...]
    feat = jnp.where(lane == 0, tau * (1.0 / (n - 1)),
                     jnp.where(lane <= bands, jnp.cos(ang), jnp.where(lane <= 2 * bands, -jnp.sin(ang), 0.0)))
    fr = fr_ref[...]
    hid = jnp.sin(fr * (jnp.dot(feat, w1_ref[...], precision=HIGHEST, preferred_element_type=f32) + b1_ref[...]))
    hid = jnp.sin(fr * (jnp.dot(hid, w2_ref[...], precision=HIGHEST, preferred_element_type=f32) + b2_ref[...]))
    hid = jnp.sin(fr * (jnp.dot(hid, w3_ref[...], precision=HIGHEST, preferred_element_type=f32) + b3_ref[...]))
    h = jnp.dot(hid, wo_ref[0, 0], precision=HIGHEST, preferred_element_type=f32)
    p, tau = taps((tp, d))
    h = h * jnp.exp(-(tau * (1.0 / (n - 1))) * dl_ref[...])
    h = jnp.where(p == n, 0.0, h)
    o_ref[0] = h

    @pl.when(i == 0)
    def _():
        ss_ref[...] = jnp.zeros_like(ss_ref)

    ss_ref[0] += jnp.sum(h * h, axis=0, keepdims=True)


def _hy_filter(n, w1, b1, w2, b2, w3, b3, freq, w_out):
    d = w_out.shape[1] // (HY_ORDER * 2)
    hid = w1.shape[1]
    tp = 256
    bands = (HY_POS_EMB - 1) // 2
    f = jnp.linspace(1e-4, bands - 1, bands, dtype=f32)
    fvec = jnp.zeros((1, 128), f32).at[0, 1:1 + bands].set(f).at[0, 1 + bands:1 + 2 * bands].set(f)
    w1p = jnp.zeros((128, hid), f32).at[:HY_POS_EMB].set(w1)
    wo = jnp.transpose(w_out.reshape(hid, HY_ORDER, 2, d), (1, 2, 0, 3))
    deltas = jnp.abs(jnp.linspace(HY_MIN_DECAY, HY_MAX_DECAY, d, dtype=f32))[None]
    small = lambda shape: pl.BlockSpec(shape, lambda o, i: (0,) * len(shape))
    return pl.pallas_call(
        functools.partial(_hy_filter_body, n=n),
        out_shape=(jax.ShapeDtypeStruct((HY_ORDER, 2 * n, d), f32), jax.ShapeDtypeStruct((HY_ORDER, 1, d), f32)),
        grid=(HY_ORDER, 2 * n // tp),
        in_specs=[small((1, 128)), small((128, hid)), small((1, hid)), small((hid, hid)), small((1, hid)),
                  small((hid, hid)), small((1, hid)), small((1, hid)),
                  pl.BlockSpec((1, 1, hid, d), lambda o, i: (o, i // (n // tp), 0, 0)), small((1, d))],
        out_specs=(pl.BlockSpec((1, tp, d), lambda o, i: (o, i, 0)), pl.BlockSpec((1, 1, d), lambda o, i: (o, 0, 0))),
        compiler_params=_cparams(("arbitrary", "arbitrary")),
        name="hy_filter",
    )(fvec, w1p, b1[None], w2, b2[None], w3, b3[None], freq[None], wo, deltas)


def _fft1_body(z_ref, m_ref, o_ref):
    half = o_ref.shape[1]

    def body(bp, carry):
        slab = jnp.concatenate([z_ref[0, :, bp, :], z_ref[1, :, bp, :]], axis=0).astype(bf16)
        r = jnp.dot(m_ref[bp], slab, preferred_element_type=f32)
        o_ref[0, :, bp, :] = r[:half]
        o_ref[1, :, bp, :] = r[half:]
        return carry

    lax.fori_loop(0, z_ref.shape[2], body, 0)


def _fft_stage1(z4, coff, m1, d):
    _, ha, bn, _ = z4.shape
    tb = 16
    return pl.pallas_call(
        _fft1_body,
        out_shape=jax.ShapeDtypeStruct((2, FFT_A, bn, d), f32),
        grid=(d // 128, bn // tb),
        in_specs=[pl.BlockSpec((2, ha, tb, 128), lambda c, j: (0, 0, j, c + coff)),
                  pl.BlockSpec((tb, 2 * FFT_A, 2 * ha), lambda c, j: (j, 0, 0))],
        out_specs=pl.BlockSpec((2, FFT_A, tb, 128), lambda c, j: (0, 0, j, c)),
        compiler_params=_cparams(("arbitrary", "arbitrary")),
        name="fft_stage1",
    )(z4, m1)


def _spec_body(t_ref, ss_ref, m2_ref, o_ref):
    scale = lax.rsqrt(ss_ref[0] + EPS)

    def body(k, carry):
        x = jnp.concatenate([t_ref[0, k], t_ref[1, k]], axis=0).astype(bf16)
        o_ref[k] = (jnp.dot(m2_ref[...], x, preferred_element_type=f32) * scale).astype(o_ref.dtype)
        return carry

    lax.fori_loop(0, t_ref.shape[1], body, 0)


def _mid_body(t_ref, h_ref, m2_ref, m3_ref, o_ref):
    bn = t_ref.shape[2]

    def body(k, carry):
        x = jnp.concatenate([t_ref[0, k], t_ref[1, k]], axis=0).astype(bf16)
        zf = jnp.dot(m2_ref[...], x, preferred_element_type=f32)
        h = h_ref[k].astype(f32)
        zr, zi, hr, hi = zf[:bn], zf[bn:], h[:bn], h[bn:]
        y = jnp.concatenate([zr * hr - zi * hi, zr * hi + zi * hr], axis=0).astype(bf16)
        tp = jnp.dot(m3_ref[...], y, preferred_element_type=f32)
        o_ref[0, k] = tp[:bn]
        o_ref[1, k] = tp[bn:]
        return carry

    lax.fori_loop(0, t_ref.shape[1], body, 0)


def _fft_mid(t, m2, m3, spec=None, ss=None):
    _, na, bn, d = t.shape
    kc = 16
    t_spec = pl.BlockSpec((2, kc, bn, 128), lambda c, j: (0, j, 0, c))
    h_spec = pl.BlockSpec((kc, 2 * bn, 128), lambda c, j: (j, 0, c))
    m_spec = pl.BlockSpec((2 * bn, 2 * bn), lambda c, j: (0, 0))
    if ss is not None:
        return pl.pallas_call(
            _spec_body,
            out_shape=jax.ShapeDtypeStruct((na, 2 * bn, d), bf16),
            grid=(d // 128, na // kc),
            in_specs=[t_spec, pl.BlockSpec((1, 1, 128), lambda c, j: (0, 0, c)), m_spec],
            out_specs=h_spec,
            compiler_params=_cparams(("arbitrary", "arbitrary")),
            name="fft_spectrum",
        )(t, ss, m2)
    return pl.pallas_call(
        _mid_body,
        out_shape=jax.ShapeDtypeStruct(t.shape, f32),
        grid=(d // 128, na // kc),
        in_specs=[t_spec, h_spec, m_spec, m_spec],
        out_specs=t_spec,
        compiler_params=_cparams(("arbitrary", "arbitrary")),
        name="fft_mid",
    )(t, spec, m2, m3)


def _ifft_body(t_ref, m_ref, z_ref, g_ref, sk_ref, o_ref):
    ha = z_ref.shape[1]

    def body(bp, carry):
        x = jnp.concatenate([t_ref[0, :, bp, :], t_ref[1, :, bp, :]], axis=0).astype(bf16)
        y = jnp.dot(m_ref[bp], x, preferred_element_type=f32)
        for s in range(2):
            o_ref[s, :, bp, :] = g_ref[s, :, bp, :] * (y[s * ha:(s + 1) * ha] + sk_ref[...] * z_ref[s, :, bp, :])
        return carry

    lax.fori_loop(0, z_ref.shape[2], body, 0)


def _fft_stage4(t, m4, z4, zoff, g4, goff, skip):
    _, na, bn, d = t.shape
    ha = z4.shape[1]
    tb = 16
    return pl.pallas_call(
        _ifft_body,
        out_shape=jax.ShapeDtypeStruct((2, ha, bn, d), f32),
        grid=(d // 128, bn // tb),
        in_specs=[pl.BlockSpec((2, na, tb, 128), lambda c, j: (0, 0, j, c)),
                  pl.BlockSpec((tb, 2 * ha, 2 * na), lambda c, j: (j, 0, 0)),
                  pl.BlockSpec((2, ha, tb, 128), lambda c, j: (0, 0, j, c + zoff)),
                  pl.BlockSpec((2, ha, tb, 128), lambda c, j: (0, 0, j, c + goff)),
                  pl.BlockSpec((1, 128), lambda c, j: (0, c))],
        out_specs=pl.BlockSpec((2, ha, tb, 128), lambda c, j: (0, 0, j, c)),
        compiler_params=_cparams(("arbitrary", "arbitrary")),
        name="fft_stage4",
    )(t, m4, z4, g4, skip)


def _ctx_spec_body(f_ref, ss_ref, mf_ref, o_ref):
    o_ref[0] = (jnp.dot(mf_ref[...], f_ref[0].astype(bf16), preferred_element_type=f32)
                * lax.rsqrt(ss_ref[0] + EPS)).astype(o_ref.dtype)


def _ctx_conv_body(z_ref, g_ref, sk_ref, h_ref, mc_ref, mi_ref, o_ref):
    n = z_ref.shape[1]
    big = 2 * n
    x = jnp.concatenate([z_ref[0], z_ref[1]], axis=0).astype(bf16)
    zf = jnp.dot(mc_ref[...], x, preferred_element_type=f32)
    h = h_ref[0].astype(f32)
    zr, zi, hr, hi = zf[:big], zf[big:], h[:big], h[big:]
    y = jnp.concatenate([zr * hr - zi * hi, zr * hi + zi * hr], axis=0).astype(bf16)
    yt = jnp.dot(mi_ref[...], y, preferred_element_type=f32)
    for s in range(2):
        o_ref[s] = g_ref[s] * (yt[s * n:(s + 1) * n] + sk_ref[...] * z_ref[s])


def _hyena_conv_small(u, full, ss, skip, d):
    bsz, n, _ = u.shape
    mf, mc, mi = _dft_tables_small(n)
    tc = 256
    spec = pl.pallas_call(
        _ctx_spec_body,
        out_shape=jax.ShapeDtypeStruct((HY_ORDER, 4 * n, d), bf16),
        grid=(HY_ORDER, d // tc),
        in_specs=[pl.BlockSpec((1, 2 * n, tc), lambda o, c: (o, 0, c)), pl.BlockSpec((1, 1, tc), lambda o, c: (o, 0, c)),
                  pl.BlockSpec((4 * n, 2 * n), lambda o, c: (0, 0))],
        out_specs=pl.BlockSpec((1, 4 * n, tc), lambda o, c: (o, 0, c)),
        compiler_params=_cparams(("arbitrary", "arbitrary")),
        name="ctx_spectrum",
    )(full, ss, mf)
    z = u
    nct = d // tc
    for o in range(HY_ORDER):
        z = pl.pallas_call(
            _ctx_conv_body,
            out_shape=jax.ShapeDtypeStruct((bsz, n, d), f32),
            grid=(nct,),
            in_specs=[pl.BlockSpec((2, n, tc), lambda c: (0, 0, c)),
                      pl.BlockSpec((2, n, tc), lambda c, o=o: (0, 0, c + (o + 1) * nct)),
                      pl.BlockSpec((1, tc), lambda c: (0, c)),
                      pl.BlockSpec((1, 4 * n, tc), lambda c, o=o: (o, 0, c)),
                      pl.BlockSpec((4 * n, 2 * n), lambda c: (0, 0)),
                      pl.BlockSpec((2 * n, 4 * n), lambda c: (0, 0))],
            out_specs=pl.BlockSpec((2, n, tc), lambda c: (0, 0, c)),
            compiler_params=_cparams(("arbitrary",)),
            name="ctx_conv",
        )(z, u, skip[o:o + 1], spec, mc, mi)
    return z


def _hyena_conv_long(u, full, ss, skip, d):
    bsz, n, c3 = u.shape
    ha = FFT_A // 2
    bn = 2 * n // FFT_A
    m1, m1f, m2, m3, m4 = _dft_tables(n)
    nct = d // 128
    u4 = u.reshape(bsz, ha, bn, c3)
    z4, zoff = u4, 0
    for o in range(HY_ORDER):
        spec = _fft_mid(_fft_stage1(full[o].reshape(2, ha, bn, d), 0, m1f, d), m2, m3, ss=ss[o:o + 1])
        t = _fft_mid(_fft_stage1(z4, zoff, m1, d), m2, m3, spec=spec)
        z4 = _fft_stage4(t, m4, z4, zoff, u4, (o + 1) * nct, skip[o:o + 1])
    return z4.reshape(bsz, n, d)


def _hyena_mixer(x, sh, sc, gate, w_in, b_in, conv_w, conv_b, f_w1, f_b1, f_w2, f_b2, f_w3, f_b3, f_freq, f_wout,
                 skip, w_out, b_out):
    bsz, n, d = x.shape
    assert bsz == 2, "the two samples are packed into one complex sequence"
    full, ss = _hy_filter(n, f_w1, f_b1, f_w2, f_b2, f_w3, f_b3, f_freq, f_wout)
    u = _hy_inproj(x, sh, sc, w_in.astype(bf16), b_in[None], conv_w, conv_b[None])
    conv = _hyena_conv_long if n % (8 * FFT_A) == 0 else _hyena_conv_small
    z = conv(u, full, ss, skip, d)
    return _matmul_residual(z, w_out.astype(bf16), b_out[None], x, gate)


def kernel(x, c, ctx, c_ctx, ada_w, ada_b, sg_w_in, sg_b_in, sg_v_gain, sg_w_s, sg_b_s, sg_w_out, sg_b_out, hy_w_in, hy_b_in, hy_conv_w, hy_conv_b, hy_f_w1, hy_f_b1, hy_f_w2, hy_f_b2, hy_f_w3, hy_f_b3, hy_f_freq, hy_f_wout, hy_skip, hy_w_out, hy_b_out, da_w_in, da_q_gain, da_k_gain, da_lam_q1, da_lam_k1, da_lam_q2, da_lam_k2, da_subln_gain, da_w_out, moe_router, moe_w_gate, moe_w_up, moe_w_down):
    bsz, n, d = x.shape
    depth = ada_w.shape[0]
    cond = jnp.zeros((8, d), f32).at[:bsz].set(c).at[bsz].set(c_ctx)
    mods = _modulation(cond, ada_w, ada_b)

    for i in range(depth):
        kind, j = i % N_MIXERS, i // N_MIXERS
        ctx_live = any(l % N_MIXERS == 2 for l in range(i + 1, depth))
        assert not (kind == 2 and ctx_live), "context-query attention is not needed at this depth and not implemented"
        mx = [mods[i, :bsz, k * d:(k + 1) * d][:, None, :] for k in range(6)]
        mc = [jnp.broadcast_to(mods[i, bsz, k * d:(k + 1) * d], (bsz, 1, d)) for k in range(6)]
        streams = [(x, mx)] + ([(ctx, mc)] if ctx_live else [])
        outs = []
        for h, m in streams:
            sh1, sc1, g1, sh2, sc2, g2 = m
            if kind == 0:
                h = _sg_mixer(h, sh1, sc1, g1, sg_w_in[j], sg_b_in[j], sg_v_gain[j], sg_w_s[j], sg_b_s[j],
                              sg_w_out[j], sg_b_out[j])
            elif kind == 1:
                h = _hyena_mixer(h, sh1, sc1, g1, hy_w_in[j], hy_b_in[j], hy_conv_w[j], hy_conv_b[j], hy_f_w1[j],
                                 hy_f_b1[j], hy_f_w2[j], hy_f_b2[j], hy_f_w3[j], hy_f_b3[j], hy_f_freq[j],
                                 hy_f_wout[j], hy_skip[j], hy_w_out[j], hy_b_out[j])
            else:
                h = _attn_mixer(h, sh1, sc1, g1, ctx, mc[0], mc[1], i, da_w_in[j], da_q_gain[j], da_k_gain[j],
                                da_lam_q1[j], da_lam_k1[j], da_lam_q2[j], da_lam_k2[j], da_subln_gain[j], da_w_out[j])
            h = _moe(h, sh2, sc2, g2, moe_router[i], i, moe_w_gate, moe_w_up, moe_w_down)
            outs.append(h)
        x = outs[0]
        if ctx_live:
            ctx = outs[1]
    return x
```
